```python
import jax, jax.numpy as jnp
from jax import lax
import numpy as np

D_MODEL = 4096
BATCH = 1
SEQ = 8192
DEPTH = 1

ATT_HEADS = 32
ATT_KV_HEADS = 4
ATT_HEAD_DIM = 64
ATT_WIDTH = ATT_HEADS * ATT_HEAD_DIM
ATT_KV_WIDTH = ATT_KV_HEADS * ATT_HEAD_DIM
ATT_GROUP = ATT_HEADS // ATT_KV_HEADS
WINDOW = 128
ATT_BLOCK = 128
ROPE_DIM = ATT_HEAD_DIM // 4
ROPE_THETA = 500000.0

RET_HEADS = 8
RET_HEAD_DIM = 256
RET_WIDTH = RET_HEADS * RET_HEAD_DIM
RET_CHUNK = 128
RET_THETA = 10000.0
RET_EPS = 1e-6

MIX_WIDTH = ATT_WIDTH + RET_WIDTH
IN_SPLITS = [ATT_WIDTH, ATT_KV_WIDTH, ATT_KV_WIDTH, RET_WIDTH, RET_WIDTH, RET_WIDTH, RET_WIDTH]
IN_COLS = int(sum(IN_SPLITS))
IN_OFFSETS = [int(o) for o in np.cumsum(IN_SPLITS)[:-1]]

D_FF = 4 * D_MODEL
NORM_EPS = 1e-5

kernel_name = "hymba_swa_sink_retention_block"


def rms_norm(x, g):
    xf = x.astype(jnp.float32)
    y = xf * lax.rsqrt(jnp.mean(xf * xf, axis=-1, keepdims=True) + NORM_EPS)
    return (y * g.astype(jnp.float32)).astype(x.dtype)


def rotate(x, ang):
    xf = x.astype(jnp.float32)
    cos = jnp.cos(ang)[None, :, None, :]
    sin = jnp.sin(ang)[None, :, None, :]
    x1, x2 = jnp.split(xf, 2, axis=-1)
    return jnp.concatenate([x1 * cos - x2 * sin, x2 * cos + x1 * sin], axis=-1).astype(x.dtype)


def partial_rope(x, pos):
    inv_freq = ROPE_THETA ** (-jnp.arange(0, ROPE_DIM, 2, dtype=jnp.float32) / ROPE_DIM)
    ang = pos[:, None] * inv_freq[None, :]
    return jnp.concatenate([rotate(x[..., :ROPE_DIM], ang), x[..., ROPE_DIM:]], axis=-1)


def sliding_window_sink_attention(q, k, v, sinks):
    B, S = q.shape[0], q.shape[1]
    nb = S // ATT_BLOCK
    qb = q.reshape(B, nb, ATT_BLOCK, ATT_KV_HEADS, ATT_GROUP, ATT_HEAD_DIM)
    pad = ((0, 0), (ATT_BLOCK, 0), (0, 0), (0, 0))
    kp = jnp.pad(k, pad).reshape(B, nb + 1, ATT_BLOCK, ATT_KV_HEADS, ATT_HEAD_DIM)
    vp = jnp.pad(v, pad).reshape(B, nb + 1, ATT_BLOCK, ATT_KV_HEADS, ATT_HEAD_DIM)
    kb = jnp.concatenate([kp[:, :-1], kp[:, 1:]], axis=2)
    vb = jnp.concatenate([vp[:, :-1], vp[:, 1:]], axis=2)
    scale = ATT_HEAD_DIM ** -0.5
    s = jnp.einsum('bnqhgd,bnkhd->bnhgqk', qb, kb,
                   preferred_element_type=jnp.float32) * scale
    n = jnp.arange(nb)[:, None, None]
    qi = jnp.arange(ATT_BLOCK)[None, :, None]
    kj = jnp.arange(2 * ATT_BLOCK)[None, None, :]
    rel = ATT_BLOCK + qi - kj
    kpos = (n - 1) * ATT_BLOCK + kj
    mask = (rel >= 0) & (rel < WINDOW) & (kpos >= 0)
    s = jnp.where(mask[None, :, None, None], s, -jnp.inf)
    sink = sinks.astype(jnp.float32).reshape(ATT_KV_HEADS, ATT_GROUP)[None, None, :, :, None, None]
    m = jnp.maximum(jnp.max(s, axis=-1, keepdims=True), sink)
    p = jnp.exp(s - m)
    w = p / (jnp.sum(p, axis=-1, keepdims=True) + jnp.exp(sink - m))
    o = jnp.einsum('bnhgqk,bnkhd->bnqhgd', w.astype(v.dtype), vb)
    return o.reshape(B, S, ATT_WIDTH)


def multiscale_retention(q, k, v, pos):
    B, S = q.shape[0], q.shape[1]
    C = RET_CHUNK
    nc = S // C
    H, Dh = RET_HEADS, RET_HEAD_DIM
    inv_freq = 1.0 / (RET_THETA ** jnp.linspace(0.0, 1.0, Dh // 2, dtype=jnp.float32))
    ang = pos[:, None] * inv_freq[None, :]
    q = rotate(q.astype(jnp.float32), ang)
    k = rotate(k.astype(jnp.float32), ang) * (Dh ** -0.5)
    v = v.astype(jnp.float32)
    log_g = jnp.log1p(-(2.0 ** (-5.0 - jnp.arange(H, dtype=jnp.float32))))
    idx = jnp.arange(C, dtype=jnp.float32)
    diff = idx[:, None] - idx[None, :]
    dmask = jnp.where(diff[None] >= 0, jnp.exp(log_g[:, None, None] * jnp.maximum(diff, 0.0)[None]), 0.0)
    zeta = jnp.exp(log_g[:, None] * (C - 1 - idx)[None])
    xi = jnp.exp(log_g[:, None] * (idx + 1)[None])
    chunk_decay = jnp.exp(log_g * C)
    qc = q.reshape(B, nc, C, H, Dh)
    kc = k.reshape(B, nc, C, H, Dh)
    vc = v.reshape(B, nc, C, H, Dh)
    a = jnp.einsum('bnihd,bnjhd->bnhij', qc, kc) * dmask[None, None]
    inner = jnp.einsum('bnhij,bnjhe->bnihe', a, vc)
    kv_chunk = jnp.einsum('bnjhd,bnjhe,hj->bnhde', kc, vc, zeta)

    def step(state, kv):
        return state * chunk_decay[None, :, None, None] + kv, state

    init = jnp.zeros((B, H, Dh, Dh), jnp.float32)
    _, prev = lax.scan(step, init, jnp.moveaxis(kv_chunk, 1, 0))
    prev = jnp.moveaxis(prev, 0, 1)
    cross = jnp.einsum('bnihd,bnhde,hi->bnihe', qc, prev, xi)
    o = (inner + cross).reshape(B, S, H, Dh)
    mu = jnp.mean(o, axis=-1, keepdims=True)
    var = jnp.mean(jnp.square(o - mu), axis=-1, keepdims=True)
    o = (o - mu) * lax.rsqrt(var + RET_EPS)
    return o.reshape(B, S, RET_WIDTH)


def setup_inputs(seed: int = 0) -> dict:
    key = jax.random.key(seed)
    ks = jax.random.split(key, 12)
    f32 = jnp.float32
    x = jax.random.normal(ks[0], (BATCH, SEQ, D_MODEL), f32)
    g_mix = 1.0 + 0.02 * jax.random.normal(ks[1], (DEPTH, D_MODEL), f32)
    w_in = jax.random.normal(ks[2], (DEPTH, D_MODEL, IN_COLS), f32) * D_MODEL ** -0.5
    att_sinks = 3.0 + jax.random.normal(ks[3], (DEPTH, ATT_HEADS), f32)
    beta_att = 1.0 + 0.02 * jax.random.normal(ks[4], (DEPTH, ATT_WIDTH), f32)
    beta_ret = 1.0 + 0.02 * jax.random.normal(ks[5], (DEPTH, RET_WIDTH), f32)
    w_out = jax.random.normal(ks[6], (DEPTH, MIX_WIDTH, D_MODEL), f32) * MIX_WIDTH ** -0.5
    g_mlp = 1.0 + 0.02 * jax.random.normal(ks[7], (DEPTH, D_MODEL), f32)
    w_up = jax.random.normal(ks[8], (DEPTH, D_MODEL, D_FF), f32) * D_MODEL ** -0.5
    w_down = jax.random.normal(ks[9], (DEPTH, D_FF, D_MODEL), f32) * D_FF ** -0.5
    g_final = 1.0 + 0.02 * jax.random.normal(ks[10], (D_MODEL,), f32)
    return {"x": x, "g_mix": g_mix, "w_in": w_in, "att_sinks": att_sinks,
            "beta_att": beta_att, "beta_ret": beta_ret, "w_out": w_out,
            "g_mlp": g_mlp, "w_up": w_up, "w_down": w_down, "g_final": g_final}


def reference(x, g_mix, w_in, att_sinks, beta_att, beta_ret, w_out, g_mlp, w_up, w_down, g_final):
    B, S, _ = x.shape
    pos = jnp.arange(S, dtype=jnp.float32)
    for l in range(DEPTH):
        h = rms_norm(x, g_mix[l])
        proj = jnp.einsum('bsd,dc->bsc', h, w_in[l])
        aq, ak, av, rq, rk, rv, rg = jnp.split(proj, IN_OFFSETS, axis=-1)
        aq = partial_rope(aq.reshape(B, S, ATT_HEADS, ATT_HEAD_DIM), pos)
        ak = partial_rope(ak.reshape(B, S, ATT_KV_HEADS, ATT_HEAD_DIM), pos)
        av = av.reshape(B, S, ATT_KV_HEADS, ATT_HEAD_DIM)
        att = sliding_window_sink_attention(aq, ak, av, att_sinks[l])
        ret = multiscale_retention(rq.reshape(B, S, RET_HEADS, RET_HEAD_DIM),
                                   rk.reshape(B, S, RET_HEADS, RET_HEAD_DIM),
                                   rv.reshape(B, S, RET_HEADS, RET_HEAD_DIM), pos)
        ret = ret.astype(x.dtype) * jax.nn.silu(rg)
        mixed = jnp.concatenate([att * beta_att[l], ret * beta_ret[l]], axis=-1)
        x = x + jnp.einsum('bsc,cd->bsd', mixed, w_out[l])
        h = rms_norm(x, g_mlp[l])
        u = jnp.square(jax.nn.relu(jnp.einsum('bsd,df->bsf', h, w_up[l])))
        x = x + jnp.einsum('bsf,fd->bsd', u, w_down[l])
    return rms_norm(x, g_final)
```

```python
import functools

import jax
import jax.numpy as jnp
from jax import lax
from jax.experimental import pallas as pl
from jax.experimental.pallas import tpu as pltpu

F32 = jnp.float32
BF16 = jnp.bfloat16

ATT_HEADS = 32
ATT_KV_HEADS = 4
ATT_HEAD_DIM = 64
ATT_GROUP = ATT_HEADS // ATT_KV_HEADS
ATT_WIDTH = ATT_HEADS * ATT_HEAD_DIM
ATT_KV_WIDTH = ATT_KV_HEADS * ATT_HEAD_DIM
ATT_BLOCK = 128
ROPE_DIM = ATT_HEAD_DIM // 4
ROPE_THETA = 500000.0
RET_HEADS = 8
RET_HEAD_DIM = 256
RET_WIDTH = RET_HEADS * RET_HEAD_DIM
RET_CHUNK = 128
RET_THETA = 10000.0
RET_EPS = 1e-6
NORM_EPS = 1e-5

LANES = 128
VMEM_LIMIT_BYTES = 56 * 1024 * 1024

OFF_AQ = 0
OFF_AK = OFF_AQ + ATT_WIDTH
OFF_AV = OFF_AK + ATT_KV_WIDTH
OFF_RQ = OFF_AV + ATT_KV_WIDTH
OFF_RK = OFF_RQ + RET_WIDTH
OFF_RV = OFF_RK + RET_WIDTH
OFF_RG = OFF_RV + RET_WIDTH


def _params(semantics):
    return pltpu.CompilerParams(dimension_semantics=semantics, vmem_limit_bytes=VMEM_LIMIT_BYTES)


def _norm_matmul_kernel(x_ref, g_ref, w_ref, o_ref, h_ref, *, act):
    @pl.when(pl.program_id(1) == 0)
    def _():
        x = x_ref[...]
        ms = jnp.mean(x * x, axis=-1, keepdims=True)
        h_ref[...] = (x * lax.rsqrt(ms + NORM_EPS) * g_ref[...]).astype(h_ref.dtype)

    acc = jnp.dot(h_ref[...], w_ref[...], preferred_element_type=F32)
    if act == "relu2":
        acc = jnp.square(jnp.maximum(acc, 0.0))
    elif act == "silu":
        acc = acc * jax.nn.sigmoid(acc)
    o_ref[...] = acc.astype(o_ref.dtype)


def _norm_matmul(x, g, w, *, col_off, n_cols, tm, tn, out_dtype, act, name):
    m, d = x.shape
    off = col_off // tn
    return pl.pallas_call(
        functools.partial(_norm_matmul_kernel, act=act),
        out_shape=jax.ShapeDtypeStruct((m, n_cols), out_dtype),
        grid=(m // tm, n_cols // tn),
        in_specs=[
            pl.BlockSpec((tm, d), lambda i, j: (i, 0)),
            pl.BlockSpec((1, d), lambda i, j: (0, 0)),
            pl.BlockSpec((d, tn), lambda i, j: (0, j + off)),
        ],
        out_specs=pl.BlockSpec((tm, tn), lambda i, j: (i, j)),
        scratch_shapes=[pltpu.VMEM((tm, d), BF16)],
        compiler_params=_params(("parallel", "arbitrary")),
        name=name,
    )(x, g, w)


def _rope_cols(x, c, s_lo, s_hi):
    return x * c + pltpu.roll(x, 8, 1) * s_hi + pltpu.roll(x, LANES - 8, 1) * s_lo


def _attn_kernel(sink_ref, q_ref, kvc_ref, kvp_ref, cc_ref, lc_ref, hc_ref, cp_ref, lp_ref, hp_ref,
                 beta_ref, o_ref):
    n = pl.program_id(0)
    blk = ATT_BLOCK
    cc, lc, hc = cc_ref[...], lc_ref[...], hc_ref[...]
    cp, lp, hp = cp_ref[...], lp_ref[...], hp_ref[...]

    scale = ATT_HEAD_DIM ** -0.5
    k_cols, v_cols = [], []
    for c in range(ATT_KV_WIDTH // LANES):
        kc = _rope_cols(kvc_ref[:, c * LANES:(c + 1) * LANES].astype(F32), cc, lc, hc)
        kp = _rope_cols(kvp_ref[:, c * LANES:(c + 1) * LANES].astype(F32), cp, lp, hp)
        k_cols.append(jnp.concatenate([kp, kc], axis=0).astype(BF16))
    k_all = jnp.concatenate(k_cols, axis=1)
    v_all = jnp.concatenate([kvp_ref[:, ATT_KV_WIDTH:], kvc_ref[:, ATT_KV_WIDTH:]], axis=0)

    qi = lax.broadcasted_iota(jnp.int32, (blk, 2 * blk), 0)
    kj = lax.broadcasted_iota(jnp.int32, (blk, 2 * blk), 1)
    rel = blk + qi - kj
    kpos = (n - 1) * blk + kj
    mask = (rel >= 0) & (rel < blk) & (kpos >= 0)

    for c in range(ATT_WIDTH // LANES):
        qc = _rope_cols(q_ref[:, c * LANES:(c + 1) * LANES].astype(F32), cc, lc, hc)
        qc = (qc * scale).astype(BF16)
        outs = []
        for half in range(2):
            head = 2 * c + half
            kvh = head // ATT_GROUP
            qh = qc[:, half * ATT_HEAD_DIM:(half + 1) * ATT_HEAD_DIM]
            kh = k_all[:, kvh * ATT_HEAD_DIM:(kvh + 1) * ATT_HEAD_DIM]
            vh = v_all[:, kvh * ATT_HEAD_DIM:(kvh + 1) * ATT_HEAD_DIM]
            s = lax.dot_general(qh, kh, (((1,), (1,)), ((), ())), preferred_element_type=F32)
            s = jnp.where(mask, s, -jnp.inf)
            sink = sink_ref[head]
            m = jnp.maximum(jnp.max(s, axis=-1, keepdims=True), sink)
            p = jnp.exp(s - m)
            denom = jnp.sum(p, axis=-1, keepdims=True) + jnp.exp(sink - m)
            o = jnp.dot(p.astype(BF16), vh, preferred_element_type=F32)
            outs.append(o / denom)
        oc = jnp.concatenate(outs, axis=1) * beta_ref[:, c * LANES:(c + 1) * LANES]
        o_ref[:, c * LANES:(c + 1) * LANES] = oc.astype(o_ref.dtype)


def _attention(proj, sinks, tabs, beta):
    s = proj.shape[0]
    blk = ATT_BLOCK
    kv_blk = OFF_AK // (2 * ATT_KV_WIDTH)
    cur = lambda n: (n, 0)
    prev = lambda n: (jnp.maximum(n - 1, 0), 0)
    tab_spec = lambda im: pl.BlockSpec((blk, LANES), im)
    return pl.pallas_call(
        _attn_kernel,
        out_shape=jax.ShapeDtypeStruct((s, ATT_WIDTH), BF16),
        grid=(s // blk,),
        in_specs=[
            pl.BlockSpec(memory_space=pltpu.SMEM),
            pl.BlockSpec((blk, ATT_WIDTH), cur),
            pl.BlockSpec((blk, 2 * ATT_KV_WIDTH), lambda n: (n, kv_blk)),
            pl.BlockSpec((blk, 2 * ATT_KV_WIDTH), lambda n: (jnp.maximum(n - 1, 0), kv_blk)),
            tab_spec(cur), tab_spec(cur), tab_spec(cur),
            tab_spec(prev), tab_spec(prev), tab_spec(prev),
            pl.BlockSpec((1, ATT_WIDTH), lambda n: (0, 0)),
        ],
        out_specs=pl.BlockSpec((blk, ATT_WIDTH), cur),
        compiler_params=_params(("parallel",)),
        name="swa_attention",
    )(sinks, proj, proj, proj, *tabs, *tabs, beta)


def _ret_kernel(lg_ref, q_ref, k_ref, v_ref, gate_ref, cos_ref, sin_ref, beta_ref, o_ref, state_ref):
    h = pl.program_id(0)
    c = pl.program_id(1)
    ch = RET_CHUNK
    dh = RET_HEAD_DIM

    @pl.when(c == 0)
    def _():
        state_ref[...] = jnp.zeros_like(state_ref)

    lg = lg_ref[h]
    cos = cos_ref[...]
    sin = sin_ref[...]

    def rot(x):
        x1, x2 = x[:, :dh // 2], x[:, dh // 2:]
        return jnp.concatenate([x1 * cos - x2 * sin, x2 * cos + x1 * sin], axis=1)

    q = rot(q_ref[...].astype(F32)).astype(BF16)
    k = rot(k_ref[...].astype(F32)) * (dh ** -0.5)
    v = v_ref[...]

    ri = lax.broadcasted_iota(jnp.int32, (ch, ch), 0)
    ci = lax.broadcasted_iota(jnp.int32, (ch, ch), 1)
    diff = (ri - ci).astype(F32)
    dmask = jnp.where(diff >= 0, jnp.exp(lg * jnp.maximum(diff, 0.0)), 0.0)
    row = lax.broadcasted_iota(jnp.int32, (ch, dh), 0).astype(F32)
    xi = jnp.exp(lg * (row + 1.0))
    zeta = jnp.exp(lg * ((ch - 1.0) - row))
    chunk_decay = jnp.exp(jnp.full((1, dh), lg * ch, F32))

    a = lax.dot_general(q, k.astype(BF16), (((1,), (1,)), ((), ())), preferred_element_type=F32) * dmask
    inner = jnp.dot(a.astype(BF16), v, preferred_element_type=F32)
    state = state_ref[...]
    cross = jnp.dot(q, state.astype(BF16), preferred_element_type=F32) * xi
    o = inner + cross

    mu = jnp.mean(o, axis=-1, keepdims=True)
    d = o - mu
    var = jnp.mean(d * d, axis=-1, keepdims=True)
    o = d * lax.rsqrt(var + RET_EPS)
    o_ref[...] = (o * gate_ref[...] * beta_ref[...]).astype(o_ref.dtype)

    kz = (k * zeta).astype(BF16)
    kv = lax.dot_general(kz, v, (((0,), (0,)), ((), ())), preferred_element_type=F32)
    state_ref[...] = state * chunk_decay + kv


def _retention(proj, gate, log_g, cos, sin, beta):
    s = proj.shape[0]
    ch, dh = RET_CHUNK, RET_HEAD_DIM
    qb, kb, vb = OFF_RQ // dh, OFF_RK // dh, OFF_RV // dh
    return pl.pallas_call(
        _ret_kernel,
        out_shape=jax.ShapeDtypeStruct((s, RET_WIDTH), BF16),
        grid=(RET_HEADS, s // ch),
        in_specs=[
            pl.BlockSpec(memory_space=pltpu.SMEM),
            pl.BlockSpec((ch, dh), lambda h, c: (c, qb + h)),
            pl.BlockSpec((ch, dh), lambda h, c: (c, kb + h)),
            pl.BlockSpec((ch, dh), lambda h, c: (c, vb + h)),
            pl.BlockSpec((ch, dh), lambda h, c: (c, h)),
            pl.BlockSpec((ch, dh // 2), lambda h, c: (c, 0)),
            pl.BlockSpec((ch, dh // 2), lambda h, c: (c, 0)),
            pl.BlockSpec((1, dh), lambda h, c: (0, h)),
        ],
        out_specs=pl.BlockSpec((ch, dh), lambda h, c: (c, h)),
        scratch_shapes=[pltpu.VMEM((dh, dh), F32)],
        compiler_params=_params(("parallel", "arbitrary")),
        name="retention",
    )(log_g, proj, proj, proj, gate, cos, sin, beta)


def _outproj_kernel(x_ref, a1_ref, a2_ref, w_ref, o_ref):
    k1 = a1_ref.shape[1]
    acc = jnp.dot(a1_ref[...], w_ref[:k1, :], preferred_element_type=F32)
    acc = acc + jnp.dot(a2_ref[...], w_ref[k1:, :], preferred_element_type=F32)
    o_ref[...] = x_ref[...] + acc


def _outproj(x, a1, a2, w, *, tm, tn):
    m, d = x.shape
    k1, k2 = a1.shape[1], a2.shape[1]
    return pl.pallas_call(
        _outproj_kernel,
        out_shape=jax.ShapeDtypeStruct((m, d), F32),
        grid=(m // tm, d // tn),
        in_specs=[
            pl.BlockSpec((tm, tn), lambda i, j: (i, j)),
            pl.BlockSpec((tm, k1), lambda i, j: (i, 0)),
            pl.BlockSpec((tm, k2), lambda i, j: (i, 0)),
            pl.BlockSpec((k1 + k2, tn), lambda i, j: (0, j)),
        ],
        out_specs=pl.BlockSpec((tm, tn), lambda i, j: (i, j)),
        compiler_params=_params(("parallel", "arbitrary")),
        name="out_proj",
    )(x, a1, a2, w)


def _down_kernel(x_ref, u_ref, w_ref, o_ref):
    k = pl.program_id(2)
    part = jnp.dot(u_ref[...], w_ref[...], preferred_element_type=F32)

    @pl.when(k == 0)
    def _():
        o_ref[...] = x_ref[...] + part

    @pl.when(k > 0)
    def _():
        o_ref[...] += part


def _down(x, u, w, *, tm, tn, tk):
    m, d = x.shape
    f = u.shape[1]
    return pl.pallas_call(
        _down_kernel,
        out_shape=jax.ShapeDtypeStruct((m, d), F32),
        grid=(m // tm, d // tn, f // tk),
        in_specs=[
            pl.BlockSpec((tm, tn), lambda i, j, k: (i, j)),
            pl.BlockSpec((tm, tk), lambda i, j, k: (i, k)),
            pl.BlockSpec((tk, tn), lambda i, j, k: (k, j)),
        ],
        out_specs=pl.BlockSpec((tm, tn), lambda i, j, k: (i, j)),
        compiler_params=_params(("parallel", "parallel", "arbitrary")),
        name="mlp_down",
    )(x, u, w)


def _rmsnorm_kernel(x_ref, g_ref, o_ref):
    x = x_ref[...]
    ms = jnp.mean(x * x, axis=-1, keepdims=True)
    o_ref[...] = x * lax.rsqrt(ms + NORM_EPS) * g_ref[...]


def _rmsnorm(x, g, *, tm):
    m, d = x.shape
    return pl.pallas_call(
        _rmsnorm_kernel,
        out_shape=jax.ShapeDtypeStruct((m, d), F32),
        grid=(m // tm,),
        in_specs=[pl.BlockSpec((tm, d), lambda i: (i, 0)), pl.BlockSpec((1, d), lambda i: (0, 0))],
        out_specs=pl.BlockSpec((tm, d), lambda i: (i, 0)),
        compiler_params=_params(("parallel",)),
        name="final_rmsnorm",
    )(x, g)


def _rope_tables(seq):
    pos = jnp.arange(seq, dtype=F32)
    inv_freq = ROPE_THETA ** (-jnp.arange(0, ROPE_DIM, 2, dtype=F32) / ROPE_DIM)
    ang = pos[:, None] * inv_freq[None, :]
    cos, sin = jnp.cos(ang), jnp.sin(ang)
    half = ROPE_DIM // 2
    rest = ATT_HEAD_DIM - ROPE_DIM
    ones = jnp.ones((seq, rest), F32)
    zeros = jnp.zeros((seq, rest), F32)
    zh = jnp.zeros((seq, half), F32)
    c = jnp.concatenate([cos, cos, ones], axis=1)
    s_lo = jnp.concatenate([-sin, zh, zeros], axis=1)
    s_hi = jnp.concatenate([zh, sin, zeros], axis=1)
    rep = LANES // ATT_HEAD_DIM
    return tuple(jnp.tile(t, (1, rep)) for t in (c, s_lo, s_hi))


def _ret_tables(seq):
    pos = jnp.arange(seq, dtype=F32)
    inv_freq = 1.0 / (RET_THETA ** jnp.linspace(0.0, 1.0, RET_HEAD_DIM // 2, dtype=F32))
    ang = pos[:, None] * inv_freq[None, :]
    return jnp.cos(ang), jnp.sin(ang)


def kernel(x, g_mix, w_in, att_sinks, beta_att, beta_ret, w_out, g_mlp, w_up, w_down, g_final):
    b, s, d = x.shape
    depth = w_in.shape[0]
    rope_tabs = _rope_tables(s)
    ret_cos, ret_sin = _ret_tables(s)
    log_g = jnp.log1p(-(2.0 ** (-5.0 - jnp.arange(RET_HEADS, dtype=F32))))

    outs = []
    for bi in range(b):
        xb = x[bi]
        for l in range(depth):
            w_in_b = w_in[l].astype(BF16)
            w_out_b = w_out[l].astype(BF16)
            w_up_b = w_up[l].astype(BF16)
            w_down_b = w_down[l].astype(BF16)
            g1 = g_mix[l][None, :]
            g2 = g_mlp[l][None, :]

            proj = _norm_matmul(xb, g1, w_in_b, col_off=0, n_cols=OFF_RG, tm=512, tn=512,
                                out_dtype=BF16, act=None, name="in_proj")
            gate = _norm_matmul(xb, g1, w_in_b, col_off=OFF_RG, n_cols=RET_WIDTH, tm=512, tn=512,
                                out_dtype=F32, act="silu", name="in_proj_gate")
            att = _attention(proj, att_sinks[l], rope_tabs, beta_att[l][None, :])
            ret = _retention(proj, gate, log_g, ret_cos, ret_sin, beta_ret[l][None, :])
            xb = _outproj(xb, att, ret, w_out_b, tm=512, tn=1024)
            u = _norm_matmul(xb, g2, w_up_b, col_off=0, n_cols=w_up_b.shape[1], tm=512, tn=1024,
                             out_dtype=BF16, act="relu2", name="mlp_up")
            xb = _down(xb, u, w_down_b, tm=1024, tn=1024, tk=2048)
        outs.append(_rmsnorm(xb, g_final[None, :], tm=512))
    return outs[0][None] if b == 1 else jnp.stack(outs, axis=0)
```

```python
import functools
import math

import jax
import jax.numpy as jnp
from jax import lax
from jax.experimental import pallas as pl
from jax.experimental.pallas import tpu as pltpu

F32 = jnp.float32
BF16 = jnp.bfloat16

ATT_HEADS = 32
ATT_KV_HEADS = 4
ATT_HEAD_DIM = 64
ATT_GROUP = ATT_HEADS // ATT_KV_HEADS
ATT_WIDTH = ATT_HEADS * ATT_HEAD_DIM
ATT_KV_WIDTH = ATT_KV_HEADS * ATT_HEAD_DIM
ATT_BLOCK = 128
ROPE_DIM = ATT_HEAD_DIM // 4
ROPE_THETA = 500000.0
RET_HEADS = 8
RET_HEAD_DIM = 256
RET_WIDTH = RET_HEADS * RET_HEAD_DIM
RET_CHUNK = 128
RET_THETA = 10000.0
RET_EPS = 1e-6
NORM_EPS = 1e-5
LOG2E = math.log2(math.e)

LANES = 128
VMEM_LIMIT_BYTES = 58 * 1024 * 1024

OFF_AQ = 0
OFF_AK = OFF_AQ + ATT_WIDTH
OFF_AV = OFF_AK + ATT_KV_WIDTH
OFF_RQ = OFF_AV + ATT_KV_WIDTH
OFF_RK = OFF_RQ + RET_WIDTH
OFF_RV = OFF_RK + RET_WIDTH
OFF_RG = OFF_RV + RET_WIDTH
IN_COLS = OFF_RG + RET_WIDTH

POS_AQ = 0
POS_RQ = POS_AQ + ATT_WIDTH
POS_RK = POS_RQ + RET_WIDTH
POS_RV = POS_RK + RET_WIDTH
POS_KV = POS_RV + RET_WIDTH
PROJ_COLS = POS_KV + 2 * ATT_KV_WIDTH

IN_TM, IN_TN = 1024, 512
OUT_TM, OUT_TN = 1024, 512
UP_TM, UP_TN = 1024, 512
DOWN_TM, DOWN_TN, DOWN_TK = 2048, 1024, 1024
ROW_TM = 512


def _params(semantics):
    return pltpu.CompilerParams(dimension_semantics=semantics, vmem_limit_bytes=VMEM_LIMIT_BYTES)


def _dot(a, b):
    return jnp.dot(a, b, preferred_element_type=F32)


def _dot_t(a, b):
    return lax.dot_general(a, b, (((1,), (1,)), ((), ())), preferred_element_type=F32)


def _row_sumsq(x):
    return jnp.broadcast_to(jnp.sum(x * x, axis=-1, keepdims=True), (x.shape[0], LANES))


def _chunks(width):
    return [slice(c * LANES, (c + 1) * LANES) for c in range(width // LANES)]


def _prenorm_kernel(x_ref, g_ref, xg_ref, r_ref):
    x = x_ref[...]
    xg_ref[...] = (x * g_ref[...]).astype(xg_ref.dtype)
    r_ref[...] = lax.rsqrt(_row_sumsq(x) * (1.0 / x.shape[1]) + NORM_EPS)


def _prenorm(x, g):
    m, d = x.shape
    tm = ROW_TM
    return pl.pallas_call(
        _prenorm_kernel,
        out_shape=[jax.ShapeDtypeStruct((m, d), BF16), jax.ShapeDtypeStruct((m, LANES), F32)],
        grid=(m // tm,),
        in_specs=[pl.BlockSpec((tm, d), lambda i: (i, 0)), pl.BlockSpec((1, d), lambda i: (0, 0))],
        out_specs=[pl.BlockSpec((tm, d), lambda i: (i, 0)), pl.BlockSpec((tm, LANES), lambda i: (i, 0))],
        compiler_params=_params(("arbitrary",)),
        name="prenorm",
    )(x, g)


def _inproj_kernel(a_ref, r_ref, w_ref, c_ref, lo_ref, hi_ref, cos_ref, sin_ref, proj_ref, gate_ref):
    j = pl.program_id(1)
    tn = w_ref.shape[1]
    t_ak = OFF_AK // tn
    t_rq = OFF_RQ // tn
    t_rk = OFF_RK // tn
    t_rv = OFF_RV // tn
    t_rg = OFF_RG // tn

    def acc_chunks():
        acc = _dot(a_ref[...], w_ref[...].astype(BF16))
        r = r_ref[...]
        return [acc[:, s] * r for s in _chunks(tn)]

    def rope(x):
        return x * c_ref[...] + pltpu.roll(x, 8, 1) * hi_ref[...] + pltpu.roll(x, LANES - 8, 1) * lo_ref[...]

    def rot_pairs(xs, scale):
        out = []
        for x1, x2 in zip(xs[0::2], xs[1::2]):
            cos, sin = cos_ref[...], sin_ref[...]
            out += [(x1 * cos - x2 * sin) * scale, (x2 * cos + x1 * sin) * scale]
        return out

    def store(ref, vals):
        for s, v in zip(_chunks(tn), vals):
            ref[:, s] = v.astype(ref.dtype)

    @pl.when(j < t_ak)
    def _():
        store(proj_ref, [rope(x) * (ATT_HEAD_DIM ** -0.5 * LOG2E) for x in acc_chunks()])

    @pl.when(j == t_ak)
    def _():
        xs = acc_chunks()
        nk = ATT_KV_WIDTH // LANES
        store(proj_ref, [rope(x) for x in xs[:nk]] + xs[nk:])

    @pl.when((j >= t_rq) & (j < t_rk))
    def _():
        store(proj_ref, rot_pairs(acc_chunks(), 1.0))

    @pl.when((j >= t_rk) & (j < t_rv))
    def _():
        store(proj_ref, rot_pairs(acc_chunks(), RET_HEAD_DIM ** -0.5))

    @pl.when((j >= t_rv) & (j < t_rg))
    def _():
        store(proj_ref, acc_chunks())

    @pl.when(j >= t_rg)
    def _():
        store(gate_ref, [x * jax.nn.sigmoid(x) for x in acc_chunks()])


def _inproj(xg, r, w, rope_tabs, ret_tabs):
    m, d = xg.shape
    tm, tn = IN_TM, IN_TN
    assert 2 * ATT_KV_WIDTH == tn
    t_ak = OFF_AK // tn
    t_rg = OFF_RG // tn

    def proj_tile(i, j):
        moved = jnp.minimum(j, t_rg - 1) - 1
        return i, jnp.where(j < t_ak, j, jnp.where(j == t_ak, POS_KV // tn, moved))

    row = lambda i, j: (i, 0)
    tab = pl.BlockSpec((tm, LANES), row)
    return pl.pallas_call(
        _inproj_kernel,
        out_shape=[jax.ShapeDtypeStruct((m, PROJ_COLS), BF16), jax.ShapeDtypeStruct((m, RET_WIDTH), F32)],
        grid=(m // tm, IN_COLS // tn),
        in_specs=[
            pl.BlockSpec((tm, d), row),
            tab,
            pl.BlockSpec((d, tn), lambda i, j: (0, j)),
            tab, tab, tab, tab, tab,
        ],
        out_specs=[
            pl.BlockSpec((tm, tn), proj_tile),
            pl.BlockSpec((tm, tn), lambda i, j: (i, jnp.maximum(j - t_rg, 0))),
        ],
        compiler_params=_params(("arbitrary", "arbitrary")),
        name="in_proj",
    )(xg, r, w, *rope_tabs, *ret_tabs)


def _attn_kernel(sink_ref, q_ref, kvc_ref, kvp_ref, beta_ref, o_ref):
    n = pl.program_id(0)
    blk = ATT_BLOCK
    hd = ATT_HEAD_DIM
    jj = lax.broadcasted_iota(jnp.int32, (blk, blk), 0)
    ii = lax.broadcasted_iota(jnp.int32, (blk, blk), 1)
    upper = jj > ii
    lo_lanes = lax.broadcasted_iota(jnp.int32, (2 * blk, LANES), 1) < hd
    prev_bias = jnp.where(n == 0, -jnp.inf, 0.0)

    v_all = jnp.concatenate([kvp_ref[:, ATT_KV_WIDTH:], kvc_ref[:, ATT_KV_WIDTH:]], axis=0)
    v_t = v_all.astype(F32).T.astype(BF16)

    for kc in range(ATT_KV_WIDTH // LANES):
        s = slice(kc * LANES, (kc + 1) * LANES)
        kcol = jnp.concatenate([kvp_ref[:, s], kvc_ref[:, s]], axis=0).astype(F32)
        kswap = pltpu.roll(kcol, hd, 1)
        for hh in range(2):
            kvh = 2 * kc + hh
            own, other = (kcol, kswap) if hh == 0 else (kswap, kcol)
            k_lo = jnp.where(lo_lanes, own, 0.0).astype(BF16)
            k_hi = jnp.where(lo_lanes, 0.0, other).astype(BF16)
            v_h = v_t[kvh * hd:(kvh + 1) * hd, :]
            for pair in range(ATT_GROUP // 2):
                col = kvh * (ATT_GROUP // 2) + pair
                cs = slice(col * LANES, (col + 1) * LANES)
                qc = q_ref[:, cs]
                outs = []
                for half, kk in enumerate((k_lo, k_hi)):
                    head = 2 * col + half
                    st = _dot_t(kk, qc)
                    comb = jnp.where(upper, st[:blk] + prev_bias, st[blk:])
                    sink = sink_ref[head] * LOG2E
                    m = jnp.maximum(jnp.max(comb, axis=0, keepdims=True), sink)
                    p = jnp.exp2(comb - m)
                    denom = jnp.sum(p, axis=0, keepdims=True) + jnp.exp2(sink - m)
                    pp = jnp.concatenate([jnp.where(upper, p, 0.0), jnp.where(upper, 0.0, p)], axis=0)
                    ot = _dot(v_h, pp.astype(BF16))
                    outs.append(ot * (1.0 / denom))
                o_pair = jnp.concatenate(outs, axis=0).T
                o_ref[:, cs] = (o_pair * beta_ref[:, cs]).astype(o_ref.dtype)


def _attention(proj, sinks, beta):
    s = proj.shape[0]
    blk = ATT_BLOCK
    kv_blk = POS_KV // (2 * ATT_KV_WIDTH)
    return pl.pallas_call(
        _attn_kernel,
        out_shape=jax.ShapeDtypeStruct((s, ATT_WIDTH), BF16),
        grid=(s // blk,),
        in_specs=[
            pl.BlockSpec(memory_space=pltpu.SMEM),
            pl.BlockSpec((blk, ATT_WIDTH), lambda n: (n, 0)),
            pl.BlockSpec((blk, 2 * ATT_KV_WIDTH), lambda n: (n, kv_blk)),
            pl.BlockSpec((blk, 2 * ATT_KV_WIDTH), lambda n: (jnp.maximum(n - 1, 0), kv_blk)),
            pl.BlockSpec((1, ATT_WIDTH), lambda n: (0, 0)),
        ],
        out_specs=pl.BlockSpec((blk, ATT_WIDTH), lambda n: (n, 0)),
        compiler_params=_params(("arbitrary",)),
        name="swa_attention",
    )(sinks, proj, proj, proj, beta)


def _ret_kernel(lg_ref, q_ref, k_ref, v_ref, gate_ref, beta_ref, o_ref,
                state_ref, dmask_ref, xi_ref, zeta_ref, decay_ref):
    c = pl.program_id(0)
    ch = RET_CHUNK
    dh = RET_HEAD_DIM

    @pl.when(c == 0)
    def _():
        state_ref[...] = jnp.zeros_like(state_ref)
        ri = lax.broadcasted_iota(jnp.int32, (ch, ch), 0)
        ci = lax.broadcasted_iota(jnp.int32, (ch, ch), 1)
        diff = (ri - ci).astype(F32)
        row = ri.astype(F32)
        for h in range(RET_HEADS):
            lg = lg_ref[h]
            dmask_ref[h] = jnp.where(diff >= 0, jnp.exp(lg * jnp.maximum(diff, 0.0)), 0.0)
            xi_ref[h] = jnp.exp(lg * (row + 1.0))
            zeta_ref[h] = jnp.exp(lg * ((ch - 1.0) - row))
            decay_ref[h] = jnp.exp(jnp.full((8, dh), lg * ch, F32))

    for h in range(RET_HEADS):
        hs = slice(h * dh, (h + 1) * dh)
        q = q_ref[:, hs]
        k = k_ref[:, hs]
        v = v_ref[:, hs]
        xi = xi_ref[h]
        zeta = zeta_ref[h]

        a = _dot_t(q, k) * dmask_ref[h]
        inner = _dot(a.astype(BF16), v)
        state = state_ref[h]
        cross = _dot(q, state.astype(BF16))
        o = inner + cross * jnp.concatenate([xi, xi], axis=1)

        mu = jnp.mean(o, axis=-1, keepdims=True)
        d = o - mu
        var = jnp.mean(d * d, axis=-1, keepdims=True)
        o = d * lax.rsqrt(var + RET_EPS)
        o_ref[:, hs] = (o * gate_ref[:, hs] * beta_ref[:, hs]).astype(o_ref.dtype)

        kz = (k.astype(F32) * jnp.concatenate([zeta, zeta], axis=1)).astype(BF16)
        kv = lax.dot_general(kz, v, (((0,), (0,)), ((), ())), preferred_element_type=F32)
        state_ref[h] = state * decay_ref[h][0:1, :] + kv


def _retention(proj, gate, log_g, beta):
    s = proj.shape[0]
    ch, dh, w = RET_CHUNK, RET_HEAD_DIM, RET_WIDTH
    blk = lambda off: pl.BlockSpec((ch, w), lambda c: (c, off // w))
    return pl.pallas_call(
        _ret_kernel,
        out_shape=jax.ShapeDtypeStruct((s, w), BF16),
        grid=(s // ch,),
        in_specs=[
            pl.BlockSpec(memory_space=pltpu.SMEM),
            blk(POS_RQ), blk(POS_RK), blk(POS_RV),
            pl.BlockSpec((ch, w), lambda c: (c, 0)),
            pl.BlockSpec((1, w), lambda c: (0, 0)),
        ],
        out_specs=pl.BlockSpec((ch, w), lambda c: (c, 0)),
        scratch_shapes=[
            pltpu.VMEM((RET_HEADS, dh, dh), F32),
            pltpu.VMEM((RET_HEADS, ch, ch), F32),
            pltpu.VMEM((RET_HEADS, ch, ch), F32),
            pltpu.VMEM((RET_HEADS, ch, ch), F32),
            pltpu.VMEM((RET_HEADS, 8, dh), F32),
        ],
        compiler_params=_params(("arbitrary",)),
        name="retention",
    )(log_g, proj, proj, proj, gate, beta)


def _outproj_kernel(x_ref, a1_ref, a2_ref, w_ref, g_ref, o_ref, xg_ref, r_ref, *, d_model):
    j = pl.program_id(1)
    k1 = a1_ref.shape[1]
    acc = _dot(a1_ref[...], w_ref[:k1, :].astype(BF16)) + _dot(a2_ref[...], w_ref[k1:, :].astype(BF16))
    x2 = x_ref[...] + acc
    o_ref[...] = x2
    xg_ref[...] = (x2 * g_ref[...]).astype(xg_ref.dtype)
    ss = _row_sumsq(x2)

    @pl.when(j == 0)
    def _():
        r_ref[...] = ss

    @pl.when(j > 0)
    def _():
        r_ref[...] += ss

    @pl.when(j == pl.num_programs(1) - 1)
    def _():
        r_ref[...] = lax.rsqrt(r_ref[...] * (1.0 / d_model) + NORM_EPS)


def _outproj(x, a1, a2, w, g):
    m, d = x.shape
    k1, k2 = a1.shape[1], a2.shape[1]
    tm, tn = OUT_TM, OUT_TN
    tile = lambda i, j: (i, j)
    row = lambda i, j: (i, 0)
    return pl.pallas_call(
        functools.partial(_outproj_kernel, d_model=d),
        out_shape=[jax.ShapeDtypeStruct((m, d), F32), jax.ShapeDtypeStruct((m, d), BF16),
                   jax.ShapeDtypeStruct((m, LANES), F32)],
        grid=(m // tm, d // tn),
        in_specs=[
            pl.BlockSpec((tm, tn), tile),
            pl.BlockSpec((tm, k1), row),
            pl.BlockSpec((tm, k2), row),
            pl.BlockSpec((k1 + k2, tn), lambda i, j: (0, j)),
            pl.BlockSpec((1, tn), lambda i, j: (0, j)),
        ],
        out_specs=[pl.BlockSpec((tm, tn), tile), pl.BlockSpec((tm, tn), tile), pl.BlockSpec((tm, LANES), row)],
        compiler_params=_params(("arbitrary", "arbitrary")),
        name="out_proj",
    )(x, a1, a2, w, g)


def _up_kernel(a_ref, r_ref, w_ref, o_ref):
    acc = _dot(a_ref[...], w_ref[...].astype(BF16))
    r = r_ref[...]
    for s in _chunks(o_ref.shape[1]):
        o_ref[:, s] = jnp.square(jnp.maximum(acc[:, s] * r, 0.0)).astype(o_ref.dtype)


def _up(xg, r, w):
    m, d = xg.shape
    f = w.shape[1]
    tm, tn = UP_TM, UP_TN
    return pl.pallas_call(
        _up_kernel,
        out_shape=jax.ShapeDtypeStruct((m, f), BF16),
        grid=(m // tm, f // tn),
        in_specs=[
            pl.BlockSpec((tm, d), lambda i, j: (i, 0)),
            pl.BlockSpec((tm, LANES), lambda i, j: (i, 0)),
            pl.BlockSpec((d, tn), lambda i, j: (0, j)),
        ],
        out_specs=pl.BlockSpec((tm, tn), lambda i, j: (i, j)),
        compiler_params=_params(("arbitrary", "arbitrary")),
        name="mlp_up",
    )(xg, r, w)


def _down_kernel(x_ref, u_ref, w_ref, o_ref, r_ref, *, d_model):
    j = pl.program_id(1)
    k = pl.program_id(2)
    last_k = k == pl.num_programs(2) - 1

    @pl.when(k == 0)
    def _():
        o_ref[...] = x_ref[...] + _dot(u_ref[...], w_ref[...].astype(BF16))

    @pl.when(k > 0)
    def _():
        o_ref[...] += _dot(u_ref[...], w_ref[...].astype(BF16))

    @pl.when(last_k & (j == 0))
    def _():
        r_ref[...] = _row_sumsq(o_ref[...])

    @pl.when(last_k & (j > 0))
    def _():
        r_ref[...] += _row_sumsq(o_ref[...])

    @pl.when(last_k & (j == pl.num_programs(1) - 1))
    def _():
        r_ref[...] = lax.rsqrt(r_ref[...] * (1.0 / d_model) + NORM_EPS)


def _down(x, u, w):
    m, d = x.shape
    f = u.shape[1]
    tm, tn, tk = DOWN_TM, DOWN_TN, DOWN_TK
    return pl.pallas_call(
        functools.partial(_down_kernel, d_model=d),
        out_shape=[jax.ShapeDtypeStruct((m, d), F32), jax.ShapeDtypeStruct((m, LANES), F32)],
        grid=(m // tm, d // tn, f // tk),
        in_specs=[
            pl.BlockSpec((tm, tn), lambda i, j, k: (i, j)),
            pl.BlockSpec((tm, tk), lambda i, j, k: (i, k)),
            pl.BlockSpec((tk, tn), lambda i, j, k: (k, j)),
        ],
        out_specs=[pl.BlockSpec((tm, tn), lambda i, j, k: (i, j)),
                   pl.BlockSpec((tm, LANES), lambda i, j, k: (i, 0))],
        compiler_params=_params(("arbitrary", "arbitrary", "arbitrary")),
        name="mlp_down",
    )(x, u, w)


def _scale_kernel(x_ref, r_ref, g_ref, o_ref):
    r = r_ref[...]
    for s in _chunks(o_ref.shape[1]):
        o_ref[:, s] = x_ref[:, s] * r * g_ref[:, s]


def _scale(x, r, g):
    m, d = x.shape
    tm = ROW_TM
    return pl.pallas_call(
        _scale_kernel,
        out_shape=jax.ShapeDtypeStruct((m, d), F32),
        grid=(m // tm,),
        in_specs=[pl.BlockSpec((tm, d), lambda i: (i, 0)), pl.BlockSpec((tm, LANES), lambda i: (i, 0)),
                  pl.BlockSpec((1, d), lambda i: (0, 0))],
        out_specs=pl.BlockSpec((tm, d), lambda i: (i, 0)),
        compiler_params=_params(("arbitrary",)),
        name="final_scale",
    )(x, r, g)


def _rope_tables(seq):
    pos = jnp.arange(seq, dtype=F32)
    inv_freq = ROPE_THETA ** (-jnp.arange(0, ROPE_DIM, 2, dtype=F32) / ROPE_DIM)
    ang = pos[:, None] * inv_freq[None, :]
    cos, sin = jnp.cos(ang), jnp.sin(ang)
    half = ROPE_DIM // 2
    rest = ATT_HEAD_DIM - ROPE_DIM
    ones = jnp.ones((seq, rest), F32)
    zeros = jnp.zeros((seq, rest), F32)
    zh = jnp.zeros((seq, half), F32)
    c = jnp.concatenate([cos, cos, ones], axis=1)
    s_lo = jnp.concatenate([-sin, zh, zeros], axis=1)
    s_hi = jnp.concatenate([zh, sin, zeros], axis=1)
    rep = LANES // ATT_HEAD_DIM
    return tuple(jnp.tile(t, (1, rep)) for t in (c, s_lo, s_hi))


def _ret_tables(seq):
    pos = jnp.arange(seq, dtype=F32)
    inv_freq = 1.0 / (RET_THETA ** jnp.linspace(0.0, 1.0, RET_HEAD_DIM // 2, dtype=F32))
    ang = pos[:, None] * inv_freq[None, :]
    return jnp.cos(ang), jnp.sin(ang)


def kernel(x, g_mix, w_in, att_sinks, beta_att, beta_ret, w_out, g_mlp, w_up, w_down, g_final):
    b, s, d = x.shape
    depth = w_in.shape[0]
    rope_tabs = _rope_tables(s)
    ret_tabs = _ret_tables(s)
    log_g = jnp.log1p(-(2.0 ** (-5.0 - jnp.arange(RET_HEADS, dtype=F32))))

    outs = []
    for bi in range(b):
        xb = x[bi]
        for l in range(depth):
            xg, r = _prenorm(xb, g_mix[l][None, :])
            proj, gate = _inproj(xg, r, w_in[l], rope_tabs, ret_tabs)
            att = _attention(proj, att_sinks[l], beta_att[l][None, :])
            ret = _retention(proj, gate, log_g, beta_ret[l][None, :])
            xb, xg, r = _outproj(xb, att, ret, w_out[l], g_mlp[l][None, :])
            u = _up(xg, r, w_up[l])
            xb, r = _down(xb, u, w_down[l])
        outs.append(_scale(xb, r, g_final[None, :]))
    return outs[0][None] if b == 1 else jnp.stack(outs, axis=0)
```

```python
import functools
import math

import jax
import jax.numpy as jnp
from jax import lax
from jax.experimental import pallas as pl
from jax.experimental.pallas import tpu as pltpu

F32 = jnp.float32
BF16 = jnp.bfloat16

ATT_HEADS = 32
ATT_KV_HEADS = 4
ATT_HEAD_DIM = 64
ATT_GROUP = ATT_HEADS // ATT_KV_HEADS
ATT_WIDTH = ATT_HEADS * ATT_HEAD_DIM
ATT_KV_WIDTH = ATT_KV_HEADS * ATT_HEAD_DIM
ATT_BLOCK = 128
ROPE_DIM = ATT_HEAD_DIM // 4
ROPE_THETA = 500000.0
RET_HEADS = 8
RET_HEAD_DIM = 256
RET_WIDTH = RET_HEADS * RET_HEAD_DIM
RET_CHUNK = 128
RET_THETA = 10000.0
RET_EPS = 1e-6
NORM_EPS = 1e-5
LOG2E = math.log2(math.e)

LANES = 128
VMEM_LIMIT_BYTES = 58 * 1024 * 1024

OFF_AQ = 0
OFF_AK = OFF_AQ + ATT_WIDTH
OFF_AV = OFF_AK + ATT_KV_WIDTH
OFF_RQ = OFF_AV + ATT_KV_WIDTH
OFF_RK = OFF_RQ + RET_WIDTH
OFF_RV = OFF_RK + RET_WIDTH
OFF_RG = OFF_RV + RET_WIDTH
IN_COLS = OFF_RG + RET_WIDTH

POS_AQ = 0
POS_RQ = POS_AQ + ATT_WIDTH
POS_RK = POS_RQ + RET_WIDTH
POS_RV = POS_RK + RET_WIDTH
POS_KV = POS_RV + RET_WIDTH
PROJ_COLS = POS_KV + 2 * ATT_KV_WIDTH

IN_TM, IN_TN = 1024, 512
OUT_TM, OUT_TN = 1024, 512
UP_TM, UP_TN = 1024, 512
DOWN_TM, DOWN_TN, DOWN_TK = 2048, 1024, 1024
ROW_TM = 512
ATT_LOOKAHEAD = 3
RET_LOOKAHEAD = 2


def _params(semantics):
    return pltpu.CompilerParams(dimension_semantics=semantics, vmem_limit_bytes=VMEM_LIMIT_BYTES)


def _dot(a, b):
    return jnp.dot(a, b, preferred_element_type=F32)


def _dot_t(a, b):
    return lax.dot_general(a, b, (((1,), (1,)), ((), ())), preferred_element_type=F32)


def _row_sumsq(x):
    return jnp.broadcast_to(jnp.sum(x * x, axis=-1, keepdims=True), (x.shape[0], LANES))


def _chunks(width):
    return [slice(c * LANES, (c + 1) * LANES) for c in range(width // LANES)]


def _prenorm_kernel(x_ref, g_ref, xg_ref, r_ref):
    x = x_ref[...]
    xg_ref[...] = (x * g_ref[...]).astype(xg_ref.dtype)
    r_ref[...] = lax.rsqrt(_row_sumsq(x) * (1.0 / x.shape[1]) + NORM_EPS)


def _prenorm(x, g):
    m, d = x.shape
    tm = ROW_TM
    return pl.pallas_call(
        _prenorm_kernel,
        out_shape=[jax.ShapeDtypeStruct((m, d), BF16), jax.ShapeDtypeStruct((m, LANES), F32)],
        grid=(m // tm,),
        in_specs=[pl.BlockSpec((tm, d), lambda i: (i, 0)), pl.BlockSpec((1, d), lambda i: (0, 0))],
        out_specs=[pl.BlockSpec((tm, d), lambda i: (i, 0)), pl.BlockSpec((tm, LANES), lambda i: (i, 0))],
        compiler_params=_params(("arbitrary",)),
        name="prenorm",
    )(x, g)


def _inproj_kernel(a_ref, r_ref, w_ref, ta_ref, tb_ref, proj_ref, gate_ref, tab_ref):
    j = pl.program_id(1)
    tn = w_ref.shape[1]
    t_ak = OFF_AK // tn
    t_rq = OFF_RQ // tn
    t_rk = OFF_RK // tn
    t_rv = OFF_RV // tn
    t_rg = OFF_RG // tn
    c_ref, s_ref, cos_ref, sin_ref = (tab_ref.at[:, s] for s in _chunks(tab_ref.shape[1]))

    @pl.when(j == 0)
    def _():
        rc_b, rs_b, rss_b, tc_b, ts_b = (tb_ref[:, s] for s in _chunks(tb_ref.shape[1]))
        for c in range(a_ref.shape[0] // LANES):
            rows = slice(c * LANES, (c + 1) * LANES)
            rc_a, rs_a, rss_a, tc_a, ts_a = (ta_ref[c:c + 1, s] for s in _chunks(ta_ref.shape[1]))
            c_ref[rows, :] = rc_a * rc_b - rs_a * rs_b
            s_ref[rows, :] = rss_a * rc_b + rc_a * rss_b
            cos_ref[rows, :] = tc_a * tc_b - ts_a * ts_b
            sin_ref[rows, :] = ts_a * tc_b + tc_a * ts_b

    def acc_chunks():
        acc = _dot(a_ref[...], w_ref[...].astype(BF16))
        r = r_ref[...]
        return [acc[:, s] * r for s in _chunks(tn)]

    def rope(x):
        lane = lax.broadcasted_iota(jnp.int32, x.shape, 1)
        first_half = (lane % ATT_HEAD_DIM) < ROPE_DIM // 2
        partner = jnp.where(first_half, pltpu.roll(x, LANES - ROPE_DIM // 2, 1), pltpu.roll(x, ROPE_DIM // 2, 1))
        return x * c_ref[...] + partner * s_ref[...]

    def rot_pairs(xs, scale):
        out = []
        for x1, x2 in zip(xs[0::2], xs[1::2]):
            cos, sin = cos_ref[...], sin_ref[...]
            out += [(x1 * cos - x2 * sin) * scale, (x2 * cos + x1 * sin) * scale]
        return out

    def store(ref, vals):
        for s, v in zip(_chunks(tn), vals):
            ref[:, s] = v.astype(ref.dtype)

    @pl.when(j < t_ak)
    def _():
        store(proj_ref, [rope(x) * (ATT_HEAD_DIM ** -0.5 * LOG2E) for x in acc_chunks()])

    @pl.when(j == t_ak)
    def _():
        xs = acc_chunks()
        nk = ATT_KV_WIDTH // LANES
        store(proj_ref, [rope(x) for x in xs[:nk]] + xs[nk:])

    @pl.when((j >= t_rq) & (j < t_rk))
    def _():
        store(proj_ref, rot_pairs(acc_chunks(), 1.0))

    @pl.when((j >= t_rk) & (j < t_rv))
    def _():
        store(proj_ref, rot_pairs(acc_chunks(), RET_HEAD_DIM ** -0.5))

    @pl.when((j >= t_rv) & (j < t_rg))
    def _():
        store(proj_ref, acc_chunks())

    @pl.when(j >= t_rg)
    def _():
        store(gate_ref, [x * jax.nn.sigmoid(x) for x in acc_chunks()])


def _inproj(xg, r, w, ta, tb):
    m, d = xg.shape
    tm, tn = IN_TM, IN_TN
    assert 2 * ATT_KV_WIDTH == tn
    assert tb.shape[0] == LANES and ta.shape[0] * LANES == m
    t_ak = OFF_AK // tn
    t_rg = OFF_RG // tn

    def proj_tile(i, j):
        moved = jnp.minimum(j, t_rg - 1) - 1
        return i, jnp.where(j < t_ak, j, jnp.where(j == t_ak, POS_KV // tn, moved))

    row = lambda i, j: (i, 0)
    return pl.pallas_call(
        _inproj_kernel,
        out_shape=[jax.ShapeDtypeStruct((m, PROJ_COLS), BF16), jax.ShapeDtypeStruct((m, RET_WIDTH), F32)],
        grid=(m // tm, IN_COLS // tn),
        in_specs=[
            pl.BlockSpec((tm, d), row),
            pl.BlockSpec((tm, LANES), row),
            pl.BlockSpec((d, tn), lambda i, j: (0, j)),
            pl.BlockSpec((tm // LANES, ta.shape[1]), row),
            pl.BlockSpec(tb.shape, lambda i, j: (0, 0)),
        ],
        out_specs=[
            pl.BlockSpec((tm, tn), proj_tile),
            pl.BlockSpec((tm, tn), lambda i, j: (i, jnp.maximum(j - t_rg, 0))),
        ],
        scratch_shapes=[pltpu.VMEM((tm, 4 * LANES), F32)],
        compiler_params=_params(("arbitrary", "arbitrary")),
        name="in_proj",
    )(xg, r, w, ta, tb)


def _attn_kernel(sink_ref, q_ref, kvc_ref, kvp_ref, beta_ref, o_ref):
    n = pl.program_id(0)
    blk = ATT_BLOCK
    hd = ATT_HEAD_DIM
    jj = lax.broadcasted_iota(jnp.int32, (blk, blk), 0)
    ii = lax.broadcasted_iota(jnp.int32, (blk, blk), 1)
    upper = jj > ii
    lo_lanes = lax.broadcasted_iota(jnp.int32, (2 * blk, LANES), 1) < hd
    prev_bias = jnp.where(n == 0, -jnp.inf, 0.0)

    v_all = jnp.concatenate([kvp_ref[:, ATT_KV_WIDTH:], kvc_ref[:, ATT_KV_WIDTH:]], axis=0)
    v_t = v_all.astype(F32).T.astype(BF16)

    keys = []
    for kc in range(ATT_KV_WIDTH // LANES):
        s = slice(kc * LANES, (kc + 1) * LANES)
        kcol = jnp.concatenate([kvp_ref[:, s], kvc_ref[:, s]], axis=0).astype(F32)
        kswap = pltpu.roll(kcol, hd, 1)
        for hh in range(2):
            own, other = (kcol, kswap) if hh == 0 else (kswap, kcol)
            keys.append((jnp.where(lo_lanes, own, 0.0).astype(BF16),
                         jnp.where(lo_lanes, 0.0, other).astype(BF16)))

    n_cols = ATT_WIDTH // LANES
    cols_per_kv = n_cols // ATT_KV_HEADS

    def scores(col):
        qc = q_ref[:, col * LANES:(col + 1) * LANES]
        return [_dot_t(kk, qc) for kk in keys[col // cols_per_kv]]

    pending = [scores(c) for c in range(ATT_LOOKAHEAD)]
    for col in range(n_cols):
        sts = pending.pop(0)
        if col + ATT_LOOKAHEAD < n_cols:
            pending.append(scores(col + ATT_LOOKAHEAD))
        kvh = col // cols_per_kv
        cs = slice(col * LANES, (col + 1) * LANES)
        v_h = v_t[kvh * hd:(kvh + 1) * hd, :]
        outs = []
        for half, st in enumerate(sts):
            head = 2 * col + half
            comb = jnp.where(upper, st[:blk] + prev_bias, st[blk:])
            sink = sink_ref[head] * LOG2E
            m = jnp.maximum(jnp.max(comb, axis=0, keepdims=True), sink)
            p = jnp.exp2(comb - m)
            denom = jnp.sum(p, axis=0, keepdims=True) + jnp.exp2(sink - m)
            pp = jnp.concatenate([jnp.where(upper, p, 0.0), jnp.where(upper, 0.0, p)], axis=0)
            ot = _dot(v_h, pp.astype(BF16))
            outs.append(ot * (1.0 / denom))
        o_pair = jnp.concatenate(outs, axis=0).T
        o_ref[:, cs] = (o_pair * beta_ref[:, cs]).astype(o_ref.dtype)


def _attention(proj, sinks, beta):
    s = proj.shape[0]
    blk = ATT_BLOCK
    kv_blk = POS_KV // (2 * ATT_KV_WIDTH)
    return pl.pallas_call(
        _attn_kernel,
        out_shape=jax.ShapeDtypeStruct((s, ATT_WIDTH), BF16),
        grid=(s // blk,),
        in_specs=[
            pl.BlockSpec(memory_space=pltpu.SMEM),
            pl.BlockSpec((blk, ATT_WIDTH), lambda n: (n, 0)),
            pl.BlockSpec((blk, 2 * ATT_KV_WIDTH), lambda n: (n, kv_blk)),
            pl.BlockSpec((blk, 2 * ATT_KV_WIDTH), lambda n: (jnp.maximum(n - 1, 0), kv_blk)),
            pl.BlockSpec((1, ATT_WIDTH), lambda n: (0, 0)),
        ],
        out_specs=pl.BlockSpec((blk, ATT_WIDTH), lambda n: (n, 0)),
        compiler_params=_params(("arbitrary",)),
        name="swa_attention",
    )(sinks, proj, proj, proj, beta)


def _ret_kernel(lg_ref, q_ref, k_ref, v_ref, gate_ref, beta_ref, o_ref,
                state_ref, dmask_ref, xi_ref, zeta_ref, decay_ref):
    c = pl.program_id(0)
    ch = RET_CHUNK
    dh = RET_HEAD_DIM

    @pl.when(c == 0)
    def _():
        state_ref[...] = jnp.zeros_like(state_ref)
        ri = lax.broadcasted_iota(jnp.int32, (ch, ch), 0)
        ci = lax.broadcasted_iota(jnp.int32, (ch, ch), 1)
        diff = (ri - ci).astype(F32)
        row = ri.astype(F32)
        for h in range(RET_HEADS):
            lg = lg_ref[h]
            dmask_ref[h] = jnp.where(diff >= 0, jnp.exp(lg * jnp.maximum(diff, 0.0)), 0.0)
            xi_ref[h] = jnp.exp(lg * (row + 1.0))
            zeta_ref[h] = jnp.exp(lg * ((ch - 1.0) - row))
            decay_ref[h] = jnp.exp(jnp.full((8, dh), lg * ch, F32))

    def first_products(h):
        hs = slice(h * dh, (h + 1) * dh)
        q = q_ref[:, hs]
        state = state_ref[h]
        return _dot_t(q, k_ref[:, hs]), _dot(q, state.astype(BF16)), state

    pending = [first_products(h) for h in range(RET_LOOKAHEAD)]
    for h in range(RET_HEADS):
        hs = slice(h * dh, (h + 1) * dh)
        qk, cross, state = pending.pop(0)
        if h + RET_LOOKAHEAD < RET_HEADS:
            pending.append(first_products(h + RET_LOOKAHEAD))
        k = k_ref[:, hs]
        v = v_ref[:, hs]
        xi = xi_ref[h]
        zeta = zeta_ref[h]

        a = qk * dmask_ref[h]
        inner = _dot(a.astype(BF16), v)
        o = inner + cross * jnp.concatenate([xi, xi], axis=1)

        mu = jnp.mean(o, axis=-1, keepdims=True)
        d = o - mu
        var = jnp.mean(d * d, axis=-1, keepdims=True)
        o = d * lax.rsqrt(var + RET_EPS)
        o_ref[:, hs] = (o * gate_ref[:, hs] * beta_ref[:, hs]).astype(o_ref.dtype)

        kz = (k.astype(F32) * jnp.concatenate([zeta, zeta], axis=1)).astype(BF16)
        kv = lax.dot_general(kz, v, (((0,), (0,)), ((), ())), preferred_element_type=F32)
        state_ref[h] = state * decay_ref[h][0:1, :] + kv


def _retention(proj, gate, log_g, beta):
    s = proj.shape[0]
    ch, dh, w = RET_CHUNK, RET_HEAD_DIM, RET_WIDTH
    blk = lambda off: pl.BlockSpec((ch, w), lambda c: (c, off // w))
    return pl.pallas_call(
        _ret_kernel,
        out_shape=jax.ShapeDtypeStruct((s, w), BF16),
        grid=(s // ch,),
        in_specs=[
            pl.BlockSpec(memory_space=pltpu.SMEM),
            blk(POS_RQ), blk(POS_RK), blk(POS_RV),
            pl.BlockSpec((ch, w), lambda c: (c, 0)),
            pl.BlockSpec((1, w), lambda c: (0, 0)),
        ],
        out_specs=pl.BlockSpec((ch, w), lambda c: (c, 0)),
        scratch_shapes=[
            pltpu.VMEM((RET_HEADS, dh, dh), F32),
            pltpu.VMEM((RET_HEADS, ch, ch), F32),
            pltpu.VMEM((RET_HEADS, ch, ch), F32),
            pltpu.VMEM((RET_HEADS, ch, ch), F32),
            pltpu.VMEM((RET_HEADS, 8, dh), F32),
        ],
        compiler_params=_params(("arbitrary",)),
        name="retention",
    )(log_g, proj, proj, proj, gate, beta)


def _outproj_kernel(x_ref, a1_ref, a2_ref, w_ref, g_ref, o_ref, xg_ref, r_ref, *, d_model):
    j = pl.program_id(1)
    k1 = a1_ref.shape[1]
    acc = _dot(a1_ref[...], w_ref[:k1, :].astype(BF16)) + _dot(a2_ref[...], w_ref[k1:, :].astype(BF16))
    x2 = x_ref[...] + acc
    o_ref[...] = x2
    xg_ref[...] = (x2 * g_ref[...]).astype(xg_ref.dtype)
    ss = _row_sumsq(x2)

    @pl.when(j == 0)
    def _():
        r_ref[...] = ss

    @pl.when(j > 0)
    def _():
        r_ref[...] += ss

    @pl.when(j == pl.num_programs(1) - 1)
    def _():
        r_ref[...] = lax.rsqrt(r_ref[...] * (1.0 / d_model) + NORM_EPS)


def _outproj(x, a1, a2, w, g):
    m, d = x.shape
    k1, k2 = a1.shape[1], a2.shape[1]
    tm, tn = OUT_TM, OUT_TN
    tile = lambda i, j: (i, j)
    row = lambda i, j: (i, 0)
    return pl.pallas_call(
        functools.partial(_outproj_kernel, d_model=d),
        out_shape=[jax.ShapeDtypeStruct((m, d), F32), jax.ShapeDtypeStruct((m, d), BF16),
                   jax.ShapeDtypeStruct((m, LANES), F32)],
        grid=(m // tm, d // tn),
        in_specs=[
            pl.BlockSpec((tm, tn), tile),
            pl.BlockSpec((tm, k1), row),
            pl.BlockSpec((tm, k2), row),
            pl.BlockSpec((k1 + k2, tn), lambda i, j: (0, j)),
            pl.BlockSpec((1, tn), lambda i, j: (0, j)),
        ],
        out_specs=[pl.BlockSpec((tm, tn), tile), pl.BlockSpec((tm, tn), tile), pl.BlockSpec((tm, LANES), row)],
        compiler_params=_params(("arbitrary", "arbitrary")),
        name="out_proj",
    )(x, a1, a2, w, g)


def _up_kernel(a_ref, r_ref, w_ref, o_ref):
    acc = _dot(a_ref[...], w_ref[...].astype(BF16))
    r = r_ref[...]
    for s in _chunks(o_ref.shape[1]):
        o_ref[:, s] = jnp.square(jnp.maximum(acc[:, s] * r, 0.0)).astype(o_ref.dtype)


def _up(xg, r, w):
    m, d = xg.shape
    f = w.shape[1]
    tm, tn = UP_TM, UP_TN
    return pl.pallas_call(
        _up_kernel,
        out_shape=jax.ShapeDtypeStruct((m, f), BF16),
        grid=(m // tm, f // tn),
        in_specs=[
            pl.BlockSpec((tm, d), lambda i, j: (i, 0)),
            pl.BlockSpec((tm, LANES), lambda i, j: (i, 0)),
            pl.BlockSpec((d, tn), lambda i, j: (0, j)),
        ],
        out_specs=pl.BlockSpec((tm, tn), lambda i, j: (i, j)),
        compiler_params=_params(("arbitrary", "arbitrary")),
        name="mlp_up",
    )(xg, r, w)


def _down_kernel(x_ref, u_ref, w_ref, o_ref):
    k = pl.program_id(2)

    @pl.when(k == 0)
    def _():
        o_ref[...] = x_ref[...] + _dot(u_ref[...], w_ref[...].astype(BF16))

    @pl.when(k > 0)
    def _():
        o_ref[...] += _dot(u_ref[...], w_ref[...].astype(BF16))


def _down(x, u, w):
    m, d = x.shape
    f = u.shape[1]
    tm, tn, tk = DOWN_TM, DOWN_TN, DOWN_TK
    return pl.pallas_call(
        _down_kernel,
        out_shape=jax.ShapeDtypeStruct((m, d), F32),
        grid=(m // tm, d // tn, f // tk),
        in_specs=[
            pl.BlockSpec((tm, tn), lambda i, j, k: (i, j)),
            pl.BlockSpec((tm, tk), lambda i, j, k: (i, k)),
            pl.BlockSpec((tk, tn), lambda i, j, k: (k, j)),
        ],
        out_specs=pl.BlockSpec((tm, tn), lambda i, j, k: (i, j)),
        compiler_params=_params(("arbitrary", "arbitrary", "arbitrary")),
        name="mlp_down",
    )(x, u, w)


def _rmsnorm_kernel(x_ref, g_ref, o_ref):
    x = x_ref[...]
    r = lax.rsqrt(_row_sumsq(x) * (1.0 / x.shape[1]) + NORM_EPS)
    for s in _chunks(o_ref.shape[1]):
        o_ref[:, s] = x[:, s] * r * g_ref[:, s]


def _rmsnorm(x, g):
    m, d = x.shape
    tm = ROW_TM
    return pl.pallas_call(
        _rmsnorm_kernel,
        out_shape=jax.ShapeDtypeStruct((m, d), F32),
        grid=(m // tm,),
        in_specs=[pl.BlockSpec((tm, d), lambda i: (i, 0)), pl.BlockSpec((1, d), lambda i: (0, 0))],
        out_specs=pl.BlockSpec((tm, d), lambda i: (i, 0)),
        compiler_params=_params(("arbitrary",)),
        name="final_rmsnorm",
    )(x, g)


def _base_tables(pos):
    n = pos.shape[0]
    half = ROPE_DIM // 2
    rest = ATT_HEAD_DIM - ROPE_DIM
    rope_freq = ROPE_THETA ** (-jnp.arange(0, ROPE_DIM, 2, dtype=F32) / ROPE_DIM)
    ang = pos[:, None] * rope_freq[None, :]
    cos, sin = jnp.cos(ang), jnp.sin(ang)
    ones, zeros = jnp.ones((n, rest), F32), jnp.zeros((n, rest), F32)
    rep = LANES // ATT_HEAD_DIM
    rope = [jnp.tile(jnp.concatenate(parts, axis=1), (1, rep)) for parts in
            ([cos, cos, ones], [sin, sin, zeros], [-sin, sin, zeros])]
    ret_freq = 1.0 / (RET_THETA ** jnp.linspace(0.0, 1.0, RET_HEAD_DIM // 2, dtype=F32))
    ang = pos[:, None] * ret_freq[None, :]
    return jnp.concatenate(rope + [jnp.cos(ang), jnp.sin(ang)], axis=1)


def kernel(x, g_mix, w_in, att_sinks, beta_att, beta_ret, w_out, g_mlp, w_up, w_down, g_final):
    b, s, d = x.shape
    depth = w_in.shape[0]
    ta = _base_tables(jnp.arange(s // LANES, dtype=F32) * LANES)
    tb = _base_tables(jnp.arange(LANES, dtype=F32))
    log_g = jnp.log1p(-(2.0 ** (-5.0 - jnp.arange(RET_HEADS, dtype=F32))))

    outs = []
    for bi in range(b):
        xb = x[bi]
        for l in range(depth):
            xg, r = _prenorm(xb, g_mix[l][None, :])
            proj, gate = _inproj(xg, r, w_in[l], ta, tb)
            att = _attention(proj, att_sinks[l], beta_att[l][None, :])
            ret = _retention(proj, gate, log_g, beta_ret[l][None, :])
            xb, xg, r = _outproj(xb, att, ret, w_out[l], g_mlp[l][None, :])
            u = _up(xg, r, w_up[l])
            xb = _down(xb, u, w_down[l])
        outs.append(_rmsnorm(xb, g_final[None, :]))
    return outs[0][None] if b == 1 else jnp.stack(outs, axis=0)
```

```python
import functools
import math

import jax
import jax.numpy as jnp
from jax import lax
from jax.experimental import pallas as pl
from jax.experimental.pallas import tpu as pltpu

F32 = jnp.float32
BF16 = jnp.bfloat16

ATT_HEADS = 32
ATT_KV_HEADS = 4
ATT_HEAD_DIM = 64
ATT_GROUP = ATT_HEADS // ATT_KV_HEADS
ATT_WIDTH = ATT_HEADS * ATT_HEAD_DIM
ATT_KV_WIDTH = ATT_KV_HEADS * ATT_HEAD_DIM
ATT_BLOCK = 128
ROPE_DIM = ATT_HEAD_DIM // 4
ROPE_THETA = 500000.0
RET_HEADS = 8
RET_HEAD_DIM = 256
RET_WIDTH = RET_HEADS * RET_HEAD_DIM
RET_CHUNK = 128
RET_THETA = 10000.0
RET_EPS = 1e-6
NORM_EPS = 1e-5
LOG2E = math.log2(math.e)

LANES = 128
VMEM_LIMIT_BYTES = 58 * 1024 * 1024

OFF_AQ = 0
OFF_AK = OFF_AQ + ATT_WIDTH
OFF_AV = OFF_AK + ATT_KV_WIDTH
OFF_RQ = OFF_AV + ATT_KV_WIDTH
OFF_RK = OFF_RQ + RET_WIDTH
OFF_RV = OFF_RK + RET_WIDTH
OFF_RG = OFF_RV + RET_WIDTH
IN_COLS = OFF_RG + RET_WIDTH

POS_AQ = 0
POS_RQ = POS_AQ + ATT_WIDTH
POS_RK = POS_RQ + RET_WIDTH
POS_RV = POS_RK + RET_WIDTH
POS_KV = POS_RV + RET_WIDTH
PROJ_COLS = POS_KV + 2 * ATT_KV_WIDTH

IN_TM, IN_TN = 1024, 512
OUT_TM, OUT_TN = 1024, 512
UP_TM, UP_TN = 1024, 512
DOWN_TM, DOWN_TN, DOWN_TK = 2048, 1024, 1024
ROW_TM = 512
ATT_LOOKAHEAD = 3
RET_LOOKAHEAD = 2
MIX_ROWS = 512


def _params(semantics):
    return pltpu.CompilerParams(dimension_semantics=semantics, vmem_limit_bytes=VMEM_LIMIT_BYTES)


def _dot(a, b):
    return jnp.dot(a, b, preferred_element_type=F32)


def _dot_t(a, b):
    return lax.dot_general(a, b, (((1,), (1,)), ((), ())), preferred_element_type=F32)


def _row_sumsq(x):
    return jnp.broadcast_to(jnp.sum(x * x, axis=-1, keepdims=True), (x.shape[0], LANES))


def _chunks(width):
    return [slice(c * LANES, (c + 1) * LANES) for c in range(width // LANES)]


def _prenorm_kernel(x_ref, g_ref, xg_ref, r_ref):
    x = x_ref[...]
    xg_ref[...] = (x * g_ref[...]).astype(xg_ref.dtype)
    r_ref[...] = lax.rsqrt(_row_sumsq(x) * (1.0 / x.shape[1]) + NORM_EPS)


def _prenorm(x, g):
    m, d = x.shape
    tm = ROW_TM
    return pl.pallas_call(
        _prenorm_kernel,
        out_shape=[jax.ShapeDtypeStruct((m, d), BF16), jax.ShapeDtypeStruct((m, LANES), F32)],
        grid=(m // tm,),
        in_specs=[pl.BlockSpec((tm, d), lambda i: (i, 0)), pl.BlockSpec((1, d), lambda i: (0, 0))],
        out_specs=[pl.BlockSpec((tm, d), lambda i: (i, 0)), pl.BlockSpec((tm, LANES), lambda i: (i, 0))],
        compiler_params=_params(("arbitrary",)),
        name="prenorm",
    )(x, g)


def _inproj_kernel(a_ref, r_ref, w_ref, ta_ref, tb_ref, bret_ref, proj_ref, gate_ref, tab_ref):
    j = pl.program_id(1)
    tn = w_ref.shape[1]
    t_ak = OFF_AK // tn
    t_rq = OFF_RQ // tn
    t_rk = OFF_RK // tn
    t_rv = OFF_RV // tn
    t_rg = OFF_RG // tn
    c_ref, s_ref, cos_ref, sin_ref = (tab_ref.at[:, s] for s in _chunks(tab_ref.shape[1]))

    @pl.when(j == 0)
    def _():
        rc_b, rs_b, rss_b, tc_b, ts_b = (tb_ref[:, s] for s in _chunks(tb_ref.shape[1]))
        for c in range(a_ref.shape[0] // LANES):
            rows = slice(c * LANES, (c + 1) * LANES)
            rc_a, rs_a, rss_a, tc_a, ts_a = (ta_ref[c:c + 1, s] for s in _chunks(ta_ref.shape[1]))
            c_ref[rows, :] = rc_a * rc_b - rs_a * rs_b
            s_ref[rows, :] = rss_a * rc_b + rc_a * rss_b
            cos_ref[rows, :] = tc_a * tc_b - ts_a * ts_b
            sin_ref[rows, :] = ts_a * tc_b + tc_a * ts_b

    def acc_chunks():
        acc = _dot(a_ref[...], w_ref[...].astype(BF16))
        r = r_ref[...]
        return [acc[:, s] * r for s in _chunks(tn)]

    def rope(x):
        lane = lax.broadcasted_iota(jnp.int32, x.shape, 1)
        first_half = (lane % ATT_HEAD_DIM) < ROPE_DIM // 2
        partner = jnp.where(first_half, pltpu.roll(x, LANES - ROPE_DIM // 2, 1), pltpu.roll(x, ROPE_DIM // 2, 1))
        return x * c_ref[...] + partner * s_ref[...]

    def rot_pairs(xs, scale):
        out = []
        for x1, x2 in zip(xs[0::2], xs[1::2]):
            cos, sin = cos_ref[...], sin_ref[...]
            out += [(x1 * cos - x2 * sin) * scale, (x2 * cos + x1 * sin) * scale]
        return out

    def store(ref, vals):
        for s, v in zip(_chunks(tn), vals):
            ref[:, s] = v.astype(ref.dtype)

    @pl.when(j < t_ak)
    def _():
        store(proj_ref, [rope(x) * (ATT_HEAD_DIM ** -0.5 * LOG2E) for x in acc_chunks()])

    @pl.when(j == t_ak)
    def _():
        xs = acc_chunks()
        nk = ATT_KV_WIDTH // LANES
        store(proj_ref, [rope(x) for x in xs[:nk]] + xs[nk:])

    @pl.when((j >= t_rq) & (j < t_rk))
    def _():
        store(proj_ref, rot_pairs(acc_chunks(), 1.0))

    @pl.when((j >= t_rk) & (j < t_rv))
    def _():
        store(proj_ref, rot_pairs(acc_chunks(), RET_HEAD_DIM ** -0.5))

    @pl.when((j >= t_rv) & (j < t_rg))
    def _():
        store(proj_ref, acc_chunks())

    @pl.when(j >= t_rg)
    def _():
        store(gate_ref, [x * jax.nn.sigmoid(x) * bret_ref[:, s] for x, s in zip(acc_chunks(), _chunks(tn))])


def _inproj(xg, r, w, ta, tb, beta_ret):
    m, d = xg.shape
    tm, tn = IN_TM, IN_TN
    assert 2 * ATT_KV_WIDTH == tn
    assert tb.shape[0] == LANES and ta.shape[0] * LANES == m
    t_ak = OFF_AK // tn
    t_rg = OFF_RG // tn

    def proj_tile(i, j):
        moved = jnp.minimum(j, t_rg - 1) - 1
        return i, jnp.where(j < t_ak, j, jnp.where(j == t_ak, POS_KV // tn, moved))

    row = lambda i, j: (i, 0)
    return pl.pallas_call(
        _inproj_kernel,
        out_shape=[jax.ShapeDtypeStruct((m, PROJ_COLS), BF16), jax.ShapeDtypeStruct((m, RET_WIDTH), F32)],
        grid=(m // tm, IN_COLS // tn),
        in_specs=[
            pl.BlockSpec((tm, d), row),
            pl.BlockSpec((tm, LANES), row),
            pl.BlockSpec((d, tn), lambda i, j: (0, j)),
            pl.BlockSpec((tm // LANES, ta.shape[1]), row),
            pl.BlockSpec(tb.shape, lambda i, j: (0, 0)),
            pl.BlockSpec((1, tn), lambda i, j: (0, jnp.maximum(j - t_rg, 0))),
        ],
        out_specs=[
            pl.BlockSpec((tm, tn), proj_tile),
            pl.BlockSpec((tm, tn), lambda i, j: (i, jnp.maximum(j - t_rg, 0))),
        ],
        scratch_shapes=[pltpu.VMEM((tm, 4 * LANES), F32)],
        compiler_params=_params(("arbitrary", "arbitrary")),
        name="in_proj",
    )(xg, r, w, ta, tb, beta_ret)


def _attn_items(first_block, sink_ref, q_ref, kvc_ref, kvp_ref, beta_ref, o_ref):
    blk = ATT_BLOCK
    hd = ATT_HEAD_DIM
    jj = lax.broadcasted_iota(jnp.int32, (blk, blk), 0)
    ii = lax.broadcasted_iota(jnp.int32, (blk, blk), 1)
    upper = jj > ii
    lo_lanes = lax.broadcasted_iota(jnp.int32, (2 * blk, LANES), 1) < hd
    prev_bias = None if first_block is None else jnp.where(first_block, -jnp.inf, 0.0)

    v_all = jnp.concatenate([kvp_ref[:, ATT_KV_WIDTH:], kvc_ref[:, ATT_KV_WIDTH:]], axis=0)
    v_t = v_all.astype(F32).T.astype(BF16)

    keys = []
    for kc in range(ATT_KV_WIDTH // LANES):
        s = slice(kc * LANES, (kc + 1) * LANES)
        kcol = jnp.concatenate([kvp_ref[:, s], kvc_ref[:, s]], axis=0).astype(F32)
        kswap = pltpu.roll(kcol, hd, 1)
        for hh in range(2):
            own, other = (kcol, kswap) if hh == 0 else (kswap, kcol)
            keys.append((jnp.where(lo_lanes, own, 0.0).astype(BF16),
                         jnp.where(lo_lanes, 0.0, other).astype(BF16)))

    n_cols = ATT_WIDTH // LANES
    cols_per_kv = n_cols // ATT_KV_HEADS

    def scores(col):
        qc = q_ref[:, col * LANES:(col + 1) * LANES]
        return [_dot_t(kk, qc) for kk in keys[col // cols_per_kv]]

    pending = [scores(c) for c in range(ATT_LOOKAHEAD)]

    def item(col):
        sts = pending.pop(0)
        if col + ATT_LOOKAHEAD < n_cols:
            pending.append(scores(col + ATT_LOOKAHEAD))
        kvh = col // cols_per_kv
        cs = slice(col * LANES, (col + 1) * LANES)
        v_h = v_t[kvh * hd:(kvh + 1) * hd, :]
        outs = []
        for half, st in enumerate(sts):
            head = 2 * col + half
            prev = st[:blk] if prev_bias is None else st[:blk] + prev_bias
            comb = jnp.where(upper, prev, st[blk:])
            sink = sink_ref[head] * LOG2E
            m = jnp.maximum(jnp.max(comb, axis=0, keepdims=True), sink)
            p = jnp.exp2(comb - m)
            denom = jnp.sum(p, axis=0, keepdims=True) + jnp.exp2(sink - m)
            pp = jnp.concatenate([jnp.where(upper, p, 0.0), jnp.where(upper, 0.0, p)], axis=0)
            ot = _dot(v_h, pp.astype(BF16))
            outs.append(ot * (1.0 / denom))
        o_pair = jnp.concatenate(outs, axis=0).T
        o_ref[:, cs] = (o_pair * beta_ref[:, cs]).astype(o_ref.dtype)

    return [functools.partial(item, col) for col in range(n_cols)]


def _ret_items(q_ref, k_ref, v_ref, gate_ref, o_ref, state_ref, dmask_ref, xi_ref, zeta_ref, decay_ref):
    dh = RET_HEAD_DIM

    def first_products(h):
        hs = slice(h * dh, (h + 1) * dh)
        q = q_ref[:, hs]
        state = state_ref[h]
        return _dot_t(q, k_ref[:, hs]), _dot(q, state.astype(BF16)), state

    pending = [first_products(h) for h in range(RET_LOOKAHEAD)]

    def item(h):
        hs = slice(h * dh, (h + 1) * dh)
        qk, cross, state = pending.pop(0)
        if h + RET_LOOKAHEAD < RET_HEADS:
            pending.append(first_products(h + RET_LOOKAHEAD))
        k = k_ref[:, hs]
        v = v_ref[:, hs]
        xi = xi_ref[h]
        zeta = zeta_ref[h]

        a = qk * dmask_ref[h]
        inner = _dot(a.astype(BF16), v)
        o = inner + cross * jnp.concatenate([xi, xi], axis=1)

        mu = jnp.mean(o, axis=-1, keepdims=True)
        d = o - mu
        var = jnp.mean(d * d, axis=-1, keepdims=True)
        o = d * lax.rsqrt(var + RET_EPS)
        o_ref[:, hs] = (o * gate_ref[:, hs]).astype(o_ref.dtype)

        kz = (k.astype(F32) * jnp.concatenate([zeta, zeta], axis=1)).astype(BF16)
        kv = lax.dot_general(kz, v, (((0,), (0,)), ((), ())), preferred_element_type=F32)
        state_ref[h] = state * decay_ref[h][0:1, :] + kv

    return [functools.partial(item, h) for h in range(RET_HEADS)]


def _mixer_kernel(sink_ref, lg_ref, aq_ref, kv_ref, kvp_ref, rq_ref, rk_ref, rv_ref, gate_ref,
                  batt_ref, o_ref, state_ref, dmask_ref, xi_ref, zeta_ref, decay_ref):
    n = pl.program_id(0)
    ch = RET_CHUNK
    dh = RET_HEAD_DIM

    @pl.when(n == 0)
    def _():
        state_ref[...] = jnp.zeros_like(state_ref)
        ri = lax.broadcasted_iota(jnp.int32, (ch, ch), 0)
        ci = lax.broadcasted_iota(jnp.int32, (ch, ch), 1)
        diff = (ri - ci).astype(F32)
        row = ri.astype(F32)
        for h in range(RET_HEADS):
            lg = lg_ref[h]
            dmask_ref[h] = jnp.where(diff >= 0, jnp.exp(lg * jnp.maximum(diff, 0.0)), 0.0)
            xi_ref[h] = jnp.exp(lg * (row + 1.0))
            zeta_ref[h] = jnp.exp(lg * ((ch - 1.0) - row))
            decay_ref[h] = jnp.exp(jnp.full((8, dh), lg * ch, F32))

    att_out = o_ref.at[:, :ATT_WIDTH]
    ret_out = o_ref.at[:, ATT_WIDTH:]
    for b in range(aq_ref.shape[0] // ch):
        rows = slice(b * ch, (b + 1) * ch)
        prev = kvp_ref if b == 0 else kv_ref.at[slice((b - 1) * ch, b * ch)]
        att = _attn_items(n == 0 if b == 0 else None, sink_ref, aq_ref.at[rows], kv_ref.at[rows], prev,
                          batt_ref, att_out.at[rows])
        ret = _ret_items(rq_ref.at[rows], rk_ref.at[rows], rv_ref.at[rows], gate_ref.at[rows],
                         ret_out.at[rows], state_ref, dmask_ref, xi_ref, zeta_ref, decay_ref)
        per_head = len(att) // len(ret)
        for h, ret_item in enumerate(ret):
            for att_item in att[h * per_head:(h + 1) * per_head]:
                att_item()
            ret_item()


def _mixer(proj, gate, sinks, log_g, beta_att):
    s = proj.shape[0]
    rows, ch, dh = MIX_ROWS, RET_CHUNK, RET_HEAD_DIM
    kv_w = 2 * ATT_KV_WIDTH
    wide = lambda pos: pl.BlockSpec((rows, RET_WIDTH), lambda n: (n, pos // RET_WIDTH))
    return pl.pallas_call(
        _mixer_kernel,
        out_shape=jax.ShapeDtypeStruct((s, ATT_WIDTH + RET_WIDTH), BF16),
        grid=(s // rows,),
        in_specs=[
            pl.BlockSpec(memory_space=pltpu.SMEM),
            pl.BlockSpec(memory_space=pltpu.SMEM),
            pl.BlockSpec((rows, ATT_WIDTH), lambda n: (n, POS_AQ // ATT_WIDTH)),
            pl.BlockSpec((rows, kv_w), lambda n: (n, POS_KV // kv_w)),
            pl.BlockSpec((ch, kv_w), lambda n: (jnp.maximum(n * (rows // ch) - 1, 0), POS_KV // kv_w)),
            wide(POS_RQ), wide(POS_RK), wide(POS_RV),
            pl.BlockSpec((rows, RET_WIDTH), lambda n: (n, 0)),
            pl.BlockSpec((1, ATT_WIDTH), lambda n: (0, 0)),
        ],
        out_specs=pl.BlockSpec((rows, ATT_WIDTH + RET_WIDTH), lambda n: (n, 0)),
        scratch_shapes=[
            pltpu.VMEM((RET_HEADS, dh, dh), F32),
            pltpu.VMEM((RET_HEADS, ch, ch), F32),
            pltpu.VMEM((RET_HEADS, ch, ch), F32),
            pltpu.VMEM((RET_HEADS, ch, ch), F32),
            pltpu.VMEM((RET_HEADS, 8, dh), F32),
        ],
        compiler_params=_params(("arbitrary",)),
        name="token_mixer",
    )(sinks, log_g, proj, proj, proj, proj, proj, proj, gate, beta_att)


def _outproj_kernel(x_ref, a_ref, w_ref, g_ref, o_ref, xg_ref, r_ref, *, d_model):
    j = pl.program_id(1)
    x2 = x_ref[...] + _dot(a_ref[...], w_ref[...].astype(BF16))
    o_ref[...] = x2
    xg_ref[...] = (x2 * g_ref[...]).astype(xg_ref.dtype)
    ss = _row_sumsq(x2)

    @pl.when(j == 0)
    def _():
        r_ref[...] = ss

    @pl.when(j > 0)
    def _():
        r_ref[...] += ss

    @pl.when(j == pl.num_programs(1) - 1)
    def _():
        r_ref[...] = lax.rsqrt(r_ref[...] * (1.0 / d_model) + NORM_EPS)


def _outproj(x, a, w, g):
    m, d = x.shape
    kdim = a.shape[1]
    tm, tn = OUT_TM, OUT_TN
    tile = lambda i, j: (i, j)
    row = lambda i, j: (i, 0)
    return pl.pallas_call(
        functools.partial(_outproj_kernel, d_model=d),
        out_shape=[jax.ShapeDtypeStruct((m, d), F32), jax.ShapeDtypeStruct((m, d), BF16),
                   jax.ShapeDtypeStruct((m, LANES), F32)],
        grid=(m // tm, d // tn),
        in_specs=[
            pl.BlockSpec((tm, tn), tile),
            pl.BlockSpec((tm, kdim), row),
            pl.BlockSpec((kdim, tn), lambda i, j: (0, j)),
            pl.BlockSpec((1, tn), lambda i, j: (0, j)),
        ],
        out_specs=[pl.BlockSpec((tm, tn), tile), pl.BlockSpec((tm, tn), tile), pl.BlockSpec((tm, LANES), row)],
        compiler_params=_params(("arbitrary", "arbitrary")),
        name="out_proj",
    )(x, a, w, g)


def _up_kernel(a_ref, r_ref, w_ref, o_ref):
    acc = _dot(a_ref[...], w_ref[...].astype(BF16))
    r = r_ref[...]
    for s in _chunks(o_ref.shape[1]):
        o_ref[:, s] = jnp.square(jnp.maximum(acc[:, s] * r, 0.0)).astype(o_ref.dtype)


def _up(xg, r, w):
    m, d = xg.shape
    f = w.shape[1]
    tm, tn = UP_TM, UP_TN
    return pl.pallas_call(
        _up_kernel,
        out_shape=jax.ShapeDtypeStruct((m, f), BF16),
        grid=(m // tm, f // tn),
        in_specs=[
            pl.BlockSpec((tm, d), lambda i, j: (i, 0)),
            pl.BlockSpec((tm, LANES), lambda i, j: (i, 0)),
            pl.BlockSpec((d, tn), lambda i, j: (0, j)),
        ],
        out_specs=pl.BlockSpec((tm, tn), lambda i, j: (i, j)),
        compiler_params=_params(("arbitrary", "arbitrary")),
        name="mlp_up",
    )(xg, r, w)


def _down_kernel(x_ref, u_ref, w_ref, o_ref):
    k = pl.program_id(2)

    @pl.when(k == 0)
    def _():
        o_ref[...] = x_ref[...]

    o_ref[...] += _dot(u_ref[...], w_ref[...].astype(BF16))


def _down(x, u, w):
    m, d = x.shape
    f = u.shape[1]
    tm, tn, tk = DOWN_TM, DOWN_TN, DOWN_TK
    return pl.pallas_call(
        _down_kernel,
        out_shape=jax.ShapeDtypeStruct((m, d), F32),
        grid=(m // tm, d // tn, f // tk),
        in_specs=[
            pl.BlockSpec((tm, tn), lambda i, j, k: (i, j)),
            pl.BlockSpec((tm, tk), lambda i, j, k: (i, k)),
            pl.BlockSpec((tk, tn), lambda i, j, k: (k, j)),
        ],
        out_specs=pl.BlockSpec((tm, tn), lambda i, j, k: (i, j)),
        compiler_params=_params(("arbitrary", "arbitrary", "arbitrary")),
        name="mlp_down",
    )(x, u, w)


def _rmsnorm_kernel(x_ref, g_ref, o_ref):
    x = x_ref[...]
    r = lax.rsqrt(_row_sumsq(x) * (1.0 / x.shape[1]) + NORM_EPS)
    for s in _chunks(o_ref.shape[1]):
        o_ref[:, s] = x[:, s] * r * g_ref[:, s]


def _rmsnorm(x, g):
    m, d = x.shape
    tm = ROW_TM
    return pl.pallas_call(
        _rmsnorm_kernel,
        out_shape=jax.ShapeDtypeStruct((m, d), F32),
        grid=(m // tm,),
        in_specs=[pl.BlockSpec((tm, d), lambda i: (i, 0)), pl.BlockSpec((1, d), lambda i: (0, 0))],
        out_specs=pl.BlockSpec((tm, d), lambda i: (i, 0)),
        compiler_params=_params(("arbitrary",)),
        name="final_rmsnorm",
    )(x, g)


def _base_tables(pos):
    n = pos.shape[0]
    rest = ATT_HEAD_DIM - ROPE_DIM
    rope_freq = ROPE_THETA ** (-jnp.arange(0, ROPE_DIM, 2, dtype=F32) / ROPE_DIM)
    ang = pos[:, None] * rope_freq[None, :]
    cos, sin = jnp.cos(ang), jnp.sin(ang)
    ones, zeros = jnp.ones((n, rest), F32), jnp.zeros((n, rest), F32)
    rep = LANES // ATT_HEAD_DIM
    rope = [jnp.tile(jnp.concatenate(parts, axis=1), (1, rep)) for parts in
            ([cos, cos, ones], [sin, sin, zeros], [-sin, sin, zeros])]
    ret_freq = 1.0 / (RET_THETA ** jnp.linspace(0.0, 1.0, RET_HEAD_DIM // 2, dtype=F32))
    ang = pos[:, None] * ret_freq[None, :]
    return jnp.concatenate(rope + [jnp.cos(ang), jnp.sin(ang)], axis=1)


def kernel(x, g_mix, w_in, att_sinks, beta_att, beta_ret, w_out, g_mlp, w_up, w_down, g_final):
    b, s, d = x.shape
    depth = w_in.shape[0]
    ta = _base_tables(jnp.arange(s // LANES, dtype=F32) * LANES)
    tb = _base_tables(jnp.arange(LANES, dtype=F32))
    log_g = jnp.log1p(-(2.0 ** (-5.0 - jnp.arange(RET_HEADS, dtype=F32))))

    outs = []
    for bi in range(b):
        xb = x[bi]
        for l in range(depth):
            xg, r = _prenorm(xb, g_mix[l][None, :])
            proj, gate = _inproj(xg, r, w_in[l], ta, tb, beta_ret[l][None, :])
            mixed = _mixer(proj, gate, att_sinks[l], log_g, beta_att[l][None, :])
            xb, xg, r = _outproj(xb, mixed, w_out[l], g_mlp[l][None, :])
            u = _up(xg, r, w_up[l])
            xb = _down(xb, u, w_down[l])
        outs.append(_rmsnorm(xb, g_final[None, :]))
    return outs[0][None] if b == 1 else jnp.stack(outs, axis=0)
```

```python
import functools
import math

import jax
import jax.numpy as jnp
from jax import lax
from jax.experimental import pallas as pl
from jax.experimental.pallas import tpu as pltpu

F32 = jnp.float32
BF16 = jnp.bfloat16

ATT_HEADS = 32
ATT_KV_HEADS = 4
ATT_HEAD_DIM = 64
ATT_GROUP = ATT_HEADS // ATT_KV_HEADS
ATT_WIDTH = ATT_HEADS * ATT_HEAD_DIM
ATT_KV_WIDTH = ATT_KV_HEADS * ATT_HEAD_DIM
ATT_BLOCK = 128
ROPE_DIM = ATT_HEAD_DIM // 4
ROPE_THETA = 500000.0
RET_HEADS = 8
RET_HEAD_DIM = 256
RET_WIDTH = RET_HEADS * RET_HEAD_DIM
RET_CHUNK = 128
RET_THETA = 10000.0
RET_EPS = 1e-6
NORM_EPS = 1e-5
LOG2E = math.log2(math.e)

LANES = 128
VMEM_LIMIT_BYTES = 58 * 1024 * 1024

OFF_AQ = 0
OFF_AK = OFF_AQ + ATT_WIDTH
OFF_AV = OFF_AK + ATT_KV_WIDTH
OFF_RQ = OFF_AV + ATT_KV_WIDTH
OFF_RK = OFF_RQ + RET_WIDTH
OFF_RV = OFF_RK + RET_WIDTH
OFF_RG = OFF_RV + RET_WIDTH
IN_COLS = OFF_RG + RET_WIDTH

POS_AQ = 0
POS_RQ = POS_AQ + ATT_WIDTH
POS_RK = POS_RQ + RET_WIDTH
POS_RV = POS_RK + RET_WIDTH
POS_KV = POS_RV + RET_WIDTH
PROJ_COLS = POS_KV + 2 * ATT_KV_WIDTH

IN_TM, IN_TN = 1024, 512
NORM_ROWS = 64
OUT_TM, OUT_TN = 1024, 512
UP_TM, UP_TN = 1024, 512
DOWN_TM, DOWN_TN, DOWN_TK = 2048, 1024, 1024
ROW_TM = 512
ATT_LOOKAHEAD = 3
RET_LOOKAHEAD = 2
MIX_ROWS = 512


def _params(semantics):
    return pltpu.CompilerParams(dimension_semantics=semantics, vmem_limit_bytes=VMEM_LIMIT_BYTES)


def _dot(a, b):
    return jnp.dot(a, b, preferred_element_type=F32)


def _dot_t(a, b):
    return lax.dot_general(a, b, (((1,), (1,)), ((), ())), preferred_element_type=F32)


def _row_sumsq(x):
    return jnp.broadcast_to(jnp.sum(x * x, axis=-1, keepdims=True), (x.shape[0], LANES))


def _chunks(width):
    return [slice(c * LANES, (c + 1) * LANES) for c in range(width // LANES)]


def _norm_operands(x_ref, g_ref):
    x = x_ref[...]
    return (x * g_ref[...]).astype(BF16), lax.rsqrt(_row_sumsq(x) * (1.0 / x.shape[1]) + NORM_EPS)


def _store_norm_slice(operands, lhs_ref, r_ref, slot, step):
    xg, r = operands
    n_slices = lhs_ref.shape[1] // NORM_ROWS
    start = pl.multiple_of(jnp.minimum(step, n_slices - 1) * NORM_ROWS, NORM_ROWS)
    rows = pl.ds(start, NORM_ROWS)
    lhs_ref[slot, rows, :] = xg
    r_ref[slot, rows, :] = r


def _norm_slice_spec(m, tm, d):
    nb = m // tm
    n_slices = tm // NORM_ROWS

    def index(i, j):
        return jnp.where(i < nb, i * n_slices + jnp.minimum(j, n_slices - 1), nb * n_slices - 1), 0

    return pl.BlockSpec((NORM_ROWS, d), index)


def _inproj_kernel(x_ref, g_ref, w_ref, ta_ref, tb_ref, bret_ref, proj_ref, gate_ref,
                   lhs_ref, r_ref, tab_ref):
    i = pl.program_id(0)
    j = pl.program_id(1)
    tm, d = lhs_ref.shape[1:]
    tn = w_ref.shape[1]
    t_ak = OFF_AK // tn
    t_rq = OFF_RQ // tn
    t_rk = OFF_RK // tn
    t_rv = OFF_RV // tn
    t_rg = OFF_RG // tn
    c_ref, s_ref, cos_ref, sin_ref = (tab_ref.at[:, s] for s in _chunks(tab_ref.shape[1]))
    mul_slot = (i + 1) % 2
    norm_slot = i % 2

    @pl.when(i == 0)
    def _():
        _store_norm_slice(_norm_operands(x_ref, g_ref), lhs_ref, r_ref, norm_slot, j)

    multiplying = i > 0

    @pl.when(multiplying & (j == 0))
    def _():
        rc_b, rs_b, rss_b, tc_b, ts_b = (tb_ref[:, s] for s in _chunks(tb_ref.shape[1]))
        for c in range(tm // LANES):
            rows = slice(c * LANES, (c + 1) * LANES)
            rc_a, rs_a, rss_a, tc_a, ts_a = (ta_ref[c:c + 1, s] for s in _chunks(ta_ref.shape[1]))
            c_ref[rows, :] = rc_a * rc_b - rs_a * rs_b
            s_ref[rows, :] = rss_a * rc_b + rc_a * rss_b
            cos_ref[rows, :] = tc_a * tc_b - ts_a * ts_b
            sin_ref[rows, :] = ts_a * tc_b + tc_a * ts_b

    def tile(ref, epilogue):
        nxt = _norm_operands(x_ref, g_ref)
        acc = _dot(lhs_ref[mul_slot], w_ref[...].astype(BF16))
        r = r_ref[mul_slot]
        vals = epilogue([acc[:, s] * r for s in _chunks(tn)])
        for s, v in zip(_chunks(tn), vals):
            ref[:, s] = v.astype(ref.dtype)
        _store_norm_slice(nxt, lhs_ref, r_ref, norm_slot, j)

    def rope(x):
        lane = lax.broadcasted_iota(jnp.int32, x.shape, 1)
        first_half = (lane % ATT_HEAD_DIM) < ROPE_DIM // 2
        partner = jnp.where(first_half, pltpu.roll(x, LANES - ROPE_DIM // 2, 1), pltpu.roll(x, ROPE_DIM // 2, 1))
        return x * c_ref[...] + partner * s_ref[...]

    def rot_pairs(xs, scale):
        out = []
        for x1, x2 in zip(xs[0::2], xs[1::2]):
            cos, sin = cos_ref[...], sin_ref[...]
            out += [(x1 * cos - x2 * sin) * scale, (x2 * cos + x1 * sin) * scale]
        return out

    @pl.when(multiplying & (j < t_ak))
    def _():
        tile(proj_ref, lambda xs: [rope(x) * (ATT_HEAD_DIM ** -0.5 * LOG2E) for x in xs])

    @pl.when(multiplying & (j == t_ak))
    def _():
        nk = ATT_KV_WIDTH // LANES
        tile(proj_ref, lambda xs: [rope(x) for x in xs[:nk]] + xs[nk:])

    @pl.when(multiplying & (j >= t_rq) & (j < t_rk))
    def _():
        tile(proj_ref, lambda xs: rot_pairs(xs, 1.0))

    @pl.when(multiplying & (j >= t_rk) & (j < t_rv))
    def _():
        tile(proj_ref, lambda xs: rot_pairs(xs, RET_HEAD_DIM ** -0.5))

    @pl.when(multiplying & (j >= t_rv) & (j < t_rg))
    def _():
        tile(proj_ref, lambda xs: xs)

    @pl.when(multiplying & (j >= t_rg))
    def _():
        tile(gate_ref, lambda xs: [x * jax.nn.sigmoid(x) * bret_ref[:, s] for x, s in zip(xs, _chunks(tn))])


def _inproj(x, g, w, ta, tb, beta_ret):
    m, d = x.shape
    tm, tn = IN_TM, IN_TN
    assert 2 * ATT_KV_WIDTH == tn
    assert tb.shape[0] == LANES and ta.shape[0] * LANES == m
    nb = m // tm
    n_tiles = IN_COLS // tn
    n_slices = tm // NORM_ROWS
    assert n_slices <= n_tiles
    t_ak = OFF_AK // tn
    t_rg = OFF_RG // tn

    def proj_tile(i, j):
        moved = jnp.minimum(j, t_rg - 1) - 1
        tile = jnp.where(j < t_ak, j, jnp.where(j == t_ak, POS_KV // tn, moved))
        return jnp.maximum(i - 1, 0), jnp.where(i == 0, 0, tile)

    def gate_tile(i, j):
        return jnp.maximum(i - 1, 0), jnp.where(i == 0, 0, jnp.maximum(j - t_rg, 0))

    return pl.pallas_call(
        _inproj_kernel,
        out_shape=[jax.ShapeDtypeStruct((m, PROJ_COLS), BF16), jax.ShapeDtypeStruct((m, RET_WIDTH), F32)],
        grid=(nb + 1, n_tiles),
        in_specs=[
            _norm_slice_spec(m, tm, d),
            pl.BlockSpec((1, d), lambda i, j: (0, 0)),
            pl.BlockSpec((d, tn), lambda i, j: (0, jnp.where(i == 0, 0, j))),
            pl.BlockSpec((tm // LANES, ta.shape[1]), lambda i, j: (jnp.maximum(i - 1, 0), 0)),
            pl.BlockSpec(tb.shape, lambda i, j: (0, 0)),
            pl.BlockSpec((1, tn), lambda i, j: (0, jnp.maximum(j - t_rg, 0))),
        ],
        out_specs=[pl.BlockSpec((tm, tn), proj_tile), pl.BlockSpec((tm, tn), gate_tile)],
        scratch_shapes=[
            pltpu.VMEM((2, tm, d), BF16),
            pltpu.VMEM((2, tm, LANES), F32),
            pltpu.VMEM((tm, 4 * LANES), F32),
        ],
        compiler_params=_params(("arbitrary", "arbitrary")),
        name="in_proj",
    )(x, g, w, ta, tb, beta_ret)


def _attn_items(first_block, sink_ref, q_ref, kvc_ref, kvp_ref, beta_ref, o_ref):
    blk = ATT_BLOCK
    hd = ATT_HEAD_DIM
    jj = lax.broadcasted_iota(jnp.int32, (blk, blk), 0)
    ii = lax.broadcasted_iota(jnp.int32, (blk, blk), 1)
    upper = jj > ii
    lo_lanes = lax.broadcasted_iota(jnp.int32, (2 * blk, LANES), 1) < hd
    prev_bias = None if first_block is None else jnp.where(first_block, -jnp.inf, 0.0)

    v_all = jnp.concatenate([kvp_ref[:, ATT_KV_WIDTH:], kvc_ref[:, ATT_KV_WIDTH:]], axis=0)
    v_t = v_all.astype(F32).T.astype(BF16)

    keys = []
    for kc in range(ATT_KV_WIDTH // LANES):
        s = slice(kc * LANES, (kc + 1) * LANES)
        kcol = jnp.concatenate([kvp_ref[:, s], kvc_ref[:, s]], axis=0).astype(F32)
        kswap = pltpu.roll(kcol, hd, 1)
        for hh in range(2):
            own, other = (kcol, kswap) if hh == 0 else (kswap, kcol)
            keys.append((jnp.where(lo_lanes, own, 0.0).astype(BF16),
                         jnp.where(lo_lanes, 0.0, other).astype(BF16)))

    n_cols = ATT_WIDTH // LANES
    cols_per_kv = n_cols // ATT_KV_HEADS

    def scores(col):
        qc = q_ref[:, col * LANES:(col + 1) * LANES]
        return [_dot_t(kk, qc) for kk in keys[col // cols_per_kv]]

    pending = [scores(c) for c in range(ATT_LOOKAHEAD)]

    def item(col):
        sts = pending.pop(0)
        if col + ATT_LOOKAHEAD < n_cols:
            pending.append(scores(col + ATT_LOOKAHEAD))
        kvh = col // cols_per_kv
        cs = slice(col * LANES, (col + 1) * LANES)
        v_h = v_t[kvh * hd:(kvh + 1) * hd, :]
        outs = []
        for half, st in enumerate(sts):
            head = 2 * col + half
            prev = st[:blk] if prev_bias is None else st[:blk] + prev_bias
            comb = jnp.where(upper, prev, st[blk:])
            sink = sink_ref[head] * LOG2E
            m = jnp.maximum(jnp.max(comb, axis=0, keepdims=True), sink)
            p = jnp.exp2(comb - m)
            denom = jnp.sum(p, axis=0, keepdims=True) + jnp.exp2(sink - m)
            pp = jnp.concatenate([jnp.where(upper, p, 0.0), jnp.where(upper, 0.0, p)], axis=0)
            ot = _dot(v_h, pp.astype(BF16))
            outs.append(ot * (1.0 / denom))
        o_pair = jnp.concatenate(outs, axis=0).T
        o_ref[:, cs] = (o_pair * beta_ref[:, cs]).astype(o_ref.dtype)

    return [functools.partial(item, col) for col in range(n_cols)]


def _ret_items(q_ref, k_ref, v_ref, gate_ref, o_ref, state_ref, dmask_ref, xi_ref, zeta_ref, decay_ref):
    dh = RET_HEAD_DIM

    def first_products(h):
        hs = slice(h * dh, (h + 1) * dh)
        q = q_ref[:, hs]
        state = state_ref[h]
        return _dot_t(q, k_ref[:, hs]), _dot(q, state.astype(BF16)), state

    pending = [first_products(h) for h in range(RET_LOOKAHEAD)]

    def item(h):
        hs = slice(h * dh, (h + 1) * dh)
        qk, cross, state = pending.pop(0)
        if h + RET_LOOKAHEAD < RET_HEADS:
            pending.append(first_products(h + RET_LOOKAHEAD))
        k = k_ref[:, hs]
        v = v_ref[:, hs]
        xi = xi_ref[h]
        zeta = zeta_ref[h]

        a = qk * dmask_ref[h]
        inner = _dot(a.astype(BF16), v)
        o = inner + cross * jnp.concatenate([xi, xi], axis=1)

        mu = jnp.mean(o, axis=-1, keepdims=True)
        d = o - mu
        var = jnp.mean(d * d, axis=-1, keepdims=True)
        o = d * lax.rsqrt(var + RET_EPS)
        o_ref[:, hs] = (o * gate_ref[:, hs]).astype(o_ref.dtype)

        kz = (k.astype(F32) * jnp.concatenate([zeta, zeta], axis=1)).astype(BF16)
        kv = lax.dot_general(kz, v, (((0,), (0,)), ((), ())), preferred_element_type=F32)
        state_ref[h] = state * decay_ref[h][0:1, :] + kv

    return [functools.partial(item, h) for h in range(RET_HEADS)]


def _mixer_kernel(sink_ref, lg_ref, aq_ref, kv_ref, kvp_ref, rq_ref, rk_ref, rv_ref, gate_ref,
                  batt_ref, o_ref, state_ref, dmask_ref, xi_ref, zeta_ref, decay_ref):
    n = pl.program_id(0)
    ch = RET_CHUNK
    dh = RET_HEAD_DIM

    @pl.when(n == 0)
    def _():
        state_ref[...] = jnp.zeros_like(state_ref)
        ri = lax.broadcasted_iota(jnp.int32, (ch, ch), 0)
        ci = lax.broadcasted_iota(jnp.int32, (ch, ch), 1)
        diff = (ri - ci).astype(F32)
        row = ri.astype(F32)
        for h in range(RET_HEADS):
            lg = lg_ref[h]
            dmask_ref[h] = jnp.where(diff >= 0, jnp.exp(lg * jnp.maximum(diff, 0.0)), 0.0)
            xi_ref[h] = jnp.exp(lg * (row + 1.0))
            zeta_ref[h] = jnp.exp(lg * ((ch - 1.0) - row))
            decay_ref[h] = jnp.exp(jnp.full((8, dh), lg * ch, F32))

    att_out = o_ref.at[:, :ATT_WIDTH]
    ret_out = o_ref.at[:, ATT_WIDTH:]
    for b in range(aq_ref.shape[0] // ch):
        rows = slice(b * ch, (b + 1) * ch)
        prev = kvp_ref if b == 0 else kv_ref.at[slice((b - 1) * ch, b * ch)]
        att = _attn_items(n == 0 if b == 0 else None, sink_ref, aq_ref.at[rows], kv_ref.at[rows], prev,
                          batt_ref, att_out.at[rows])
        ret = _ret_items(rq_ref.at[rows], rk_ref.at[rows], rv_ref.at[rows], gate_ref.at[rows],
                         ret_out.at[rows], state_ref, dmask_ref, xi_ref, zeta_ref, decay_ref)
        per_head = len(att) // len(ret)
        for h, ret_item in enumerate(ret):
            for att_item in att[h * per_head:(h + 1) * per_head]:
                att_item()
            ret_item()


def _mixer(proj, gate, sinks, log_g, beta_att):
    s = proj.shape[0]
    rows, ch, dh = MIX_ROWS, RET_CHUNK, RET_HEAD_DIM
    kv_w = 2 * ATT_KV_WIDTH
    wide = lambda pos: pl.BlockSpec((rows, RET_WIDTH), lambda n: (n, pos // RET_WIDTH))
    return pl.pallas_call(
        _mixer_kernel,
        out_shape=jax.ShapeDtypeStruct((s, ATT_WIDTH + RET_WIDTH), BF16),
        grid=(s // rows,),
        in_specs=[
            pl.BlockSpec(memory_space=pltpu.SMEM),
            pl.BlockSpec(memory_space=pltpu.SMEM),
            pl.BlockSpec((rows, ATT_WIDTH), lambda n: (n, POS_AQ // ATT_WIDTH)),
            pl.BlockSpec((rows, kv_w), lambda n: (n, POS_KV // kv_w)),
            pl.BlockSpec((ch, kv_w), lambda n: (jnp.maximum(n * (rows // ch) - 1, 0), POS_KV // kv_w)),
            wide(POS_RQ), wide(POS_RK), wide(POS_RV),
            pl.BlockSpec((rows, RET_WIDTH), lambda n: (n, 0)),
            pl.BlockSpec((1, ATT_WIDTH), lambda n: (0, 0)),
        ],
        out_specs=pl.BlockSpec((rows, ATT_WIDTH + RET_WIDTH), lambda n: (n, 0)),
        scratch_shapes=[
            pltpu.VMEM((RET_HEADS, dh, dh), F32),
            pltpu.VMEM((RET_HEADS, ch, ch), F32),
            pltpu.VMEM((RET_HEADS, ch, ch), F32),
            pltpu.VMEM((RET_HEADS, ch, ch), F32),
            pltpu.VMEM((RET_HEADS, 8, dh), F32),
        ],
        compiler_params=_params(("arbitrary",)),
        name="token_mixer",
    )(sinks, log_g, proj, proj, proj, proj, proj, proj, gate, beta_att)


def _outproj_kernel(x_ref, a_ref, w_ref, g_ref, o_ref, xg_ref, r_ref, *, d_model):
    j = pl.program_id(1)
    x2 = x_ref[...] + _dot(a_ref[...], w_ref[...].astype(BF16))
    o_ref[...] = x2
    xg_ref[...] = (x2 * g_ref[...]).astype(xg_ref.dtype)
    ss = _row_sumsq(x2)

    @pl.when(j == 0)
    def _():
        r_ref[...] = ss

    @pl.when(j > 0)
    def _():
        r_ref[...] += ss

    @pl.when(j == pl.num_programs(1) - 1)
    def _():
        r_ref[...] = lax.rsqrt(r_ref[...] * (1.0 / d_model) + NORM_EPS)


def _outproj(x, a, w, g):
    m, d = x.shape
    kdim = a.shape[1]
    tm, tn = OUT_TM, OUT_TN
    tile = lambda i, j: (i, j)
    row = lambda i, j: (i, 0)
    return pl.pallas_call(
        functools.partial(_outproj_kernel, d_model=d),
        out_shape=[jax.ShapeDtypeStruct((m, d), F32), jax.ShapeDtypeStruct((m, d), BF16),
                   jax.ShapeDtypeStruct((m, LANES), F32)],
        grid=(m // tm, d // tn),
        in_specs=[
            pl.BlockSpec((tm, tn), tile),
            pl.BlockSpec((tm, kdim), row),
            pl.BlockSpec((kdim, tn), lambda i, j: (0, j)),
            pl.BlockSpec((1, tn), lambda i, j: (0, j)),
        ],
        out_specs=[pl.BlockSpec((tm, tn), tile), pl.BlockSpec((tm, tn), tile), pl.BlockSpec((tm, LANES), row)],
        compiler_params=_params(("arbitrary", "arbitrary")),
        name="out_proj",
    )(x, a, w, g)


def _up_kernel(a_ref, r_ref, w_ref, o_ref):
    acc = _dot(a_ref[...], w_ref[...].astype(BF16))
    r = r_ref[...]
    for s in _chunks(o_ref.shape[1]):
        o_ref[:, s] = jnp.square(jnp.maximum(acc[:, s] * r, 0.0)).astype(o_ref.dtype)


def _up(xg, r, w):
    m, d = xg.shape
    f = w.shape[1]
    tm, tn = UP_TM, UP_TN
    return pl.pallas_call(
        _up_kernel,
        out_shape=jax.ShapeDtypeStruct((m, f), BF16),
        grid=(m // tm, f // tn),
        in_specs=[
            pl.BlockSpec((tm, d), lambda i, j: (i, 0)),
            pl.BlockSpec((tm, LANES), lambda i, j: (i, 0)),
            pl.BlockSpec((d, tn), lambda i, j: (0, j)),
        ],
        out_specs=pl.BlockSpec((tm, tn), lambda i, j: (i, j)),
        compiler_params=_params(("arbitrary", "arbitrary")),
        name="mlp_up",
    )(xg, r, w)


def _down_kernel(x_ref, u_ref, w_ref, o_ref):
    k = pl.program_id(2)

    @pl.when(k == 0)
    def _():
        o_ref[...] = x_ref[...]

    o_ref[...] += _dot(u_ref[...], w_ref[...].astype(BF16))


def _down(x, u, w):
    m, d = x.shape
    f = u.shape[1]
    tm, tn, tk = DOWN_TM, DOWN_TN, DOWN_TK
    return pl.pallas_call(
        _down_kernel,
        out_shape=jax.ShapeDtypeStruct((m, d), F32),
        grid=(m // tm, d // tn, f // tk),
        in_specs=[
            pl.BlockSpec((tm, tn), lambda i, j, k: (i, j)),
            pl.BlockSpec((tm, tk), lambda i, j, k: (i, k)),
            pl.BlockSpec((tk, tn), lambda i, j, k: (k, j)),
        ],
        out_specs=pl.BlockSpec((tm, tn), lambda i, j, k: (i, j)),
        compiler_params=_params(("arbitrary", "arbitrary", "arbitrary")),
        name="mlp_down",
    )(x, u, w)


def _rmsnorm_kernel(x_ref, g_ref, o_ref):
    x = x_ref[...]
    r = lax.rsqrt(_row_sumsq(x) * (1.0 / x.shape[1]) + NORM_EPS)
    for s in _chunks(o_ref.shape[1]):
        o_ref[:, s] = x[:, s] * r * g_ref[:, s]


def _rmsnorm(x, g):
    m, d = x.shape
    tm = ROW_TM
    return pl.pallas_call(
        _rmsnorm_kernel,
        out_shape=jax.ShapeDtypeStruct((m, d), F32),
        grid=(m // tm,),
        in_specs=[pl.BlockSpec((tm, d), lambda i: (i, 0)), pl.BlockSpec((1, d), lambda i: (0, 0))],
        out_specs=pl.BlockSpec((tm, d), lambda i: (i, 0)),
        compiler_params=_params(("arbitrary",)),
        name="final_rmsnorm",
    )(x, g)


def _base_tables(pos):
    n = pos.shape[0]
    rest = ATT_HEAD_DIM - ROPE_DIM
    rope_freq = ROPE_THETA ** (-jnp.arange(0, ROPE_DIM, 2, dtype=F32) / ROPE_DIM)
    ang = pos[:, None] * rope_freq[None, :]
    cos, sin = jnp.cos(ang), jnp.sin(ang)
    ones, zeros = jnp.ones((n, rest), F32), jnp.zeros((n, rest), F32)
    rep = LANES // ATT_HEAD_DIM
    rope = [jnp.tile(jnp.concatenate(parts, axis=1), (1, rep)) for parts in
            ([cos, cos, ones], [sin, sin, zeros], [-sin, sin, zeros])]
    ret_freq = 1.0 / (RET_THETA ** jnp.linspace(0.0, 1.0, RET_HEAD_DIM // 2, dtype=F32))
    ang = pos[:, None] * ret_freq[None, :]
    return jnp.concatenate(rope + [jnp.cos(ang), jnp.sin(ang)], axis=1)


def kernel(x, g_mix, w_in, att_sinks, beta_att, beta_ret, w_out, g_mlp, w_up, w_down, g_final):
    b, s, d = x.shape
    depth = w_in.shape[0]
    ta = _base_tables(jnp.arange(s // LANES, dtype=F32) * LANES)
    tb = _base_tables(jnp.arange(LANES, dtype=F32))
    log_g = jnp.log1p(-(2.0 ** (-5.0 - jnp.arange(RET_HEADS, dtype=F32))))

    outs = []
    for bi in range(b):
        xb = x[bi]
        for l in range(depth):
            proj, gate = _inproj(xb, g_mix[l][None, :], w_in[l], ta, tb, beta_ret[l][None, :])
            mixed = _mixer(proj, gate, att_sinks[l], log_g, beta_att[l][None, :])
            xb, xg, r = _outproj(xb, mixed, w_out[l], g_mlp[l][None, :])
            u = _up(xg, r, w_up[l])
            xb = _down(xb, u, w_down[l])
        outs.append(_rmsnorm(xb, g_final[None, :]))
    return outs[0][None] if b == 1 else jnp.stack(outs, axis=0)
```

```python
import functools
import math

import jax
import jax.numpy as jnp
from jax import lax
from jax.experimental import pallas as pl
from jax.experimental.pallas import tpu as pltpu

F32 = jnp.float32
BF16 = jnp.bfloat16

ATT_HEADS = 32
ATT_KV_HEADS = 4
ATT_HEAD_DIM = 64
ATT_GROUP = ATT_HEADS // ATT_KV_HEADS
ATT_WIDTH = ATT_HEADS * ATT_HEAD_DIM
ATT_KV_WIDTH = ATT_KV_HEADS * ATT_HEAD_DIM
ATT_BLOCK = 128
ROPE_DIM = ATT_HEAD_DIM // 4
ROPE_THETA = 500000.0
RET_HEADS = 8
RET_HEAD_DIM = 256
RET_WIDTH = RET_HEADS * RET_HEAD_DIM
RET_CHUNK = 128
RET_THETA = 10000.0
RET_EPS = 1e-6
NORM_EPS = 1e-5
LOG2E = math.log2(math.e)

LANES = 128
SUBLANES = 8
VMEM_LIMIT_BYTES = 58 * 1024 * 1024

OFF_AQ = 0
OFF_AK = OFF_AQ + ATT_WIDTH
OFF_AV = OFF_AK + ATT_KV_WIDTH
OFF_RQ = OFF_AV + ATT_KV_WIDTH
OFF_RK = OFF_RQ + RET_WIDTH
OFF_RV = OFF_RK + RET_WIDTH
OFF_RG = OFF_RV + RET_WIDTH
IN_COLS = OFF_RG + RET_WIDTH

POS_AQ = 0
POS_RQ = POS_AQ + ATT_WIDTH
POS_RK = POS_RQ + RET_WIDTH
POS_RV = POS_RK + RET_WIDTH
POS_KV = POS_RV + RET_WIDTH
PROJ_COLS = POS_KV + 2 * ATT_KV_WIDTH

IN_TM, IN_TN = 1024, 512
NORM_ROWS = 64
OUT_TM, OUT_TN = 1024, 512
UP_TM, UP_TN = 1024, 512
DOWN_TM, DOWN_TN, DOWN_TK = 2048, 1024, 1024
ROW_TM = 512
ATT_LOOKAHEAD = 3
RET_LOOKAHEAD = 2
MIX_ROWS = 512


def _params(semantics):
    return pltpu.CompilerParams(dimension_semantics=semantics, vmem_limit_bytes=VMEM_LIMIT_BYTES)


def _dot(a, b):
    return jnp.dot(a, b, preferred_element_type=F32)


def _dot_t(a, b):
    return lax.dot_general(a, b, (((1,), (1,)), ((), ())), preferred_element_type=F32)


def _row_sumsq(x):
    return jnp.broadcast_to(jnp.sum(x * x, axis=-1, keepdims=True), (x.shape[0], LANES))


def _chunks(width):
    return [slice(c * LANES, (c + 1) * LANES) for c in range(width // LANES)]


def _norm_operands(x_ref, g_ref):
    x = x_ref[...]
    return (x * g_ref[...]).astype(BF16), lax.rsqrt(_row_sumsq(x) * (1.0 / x.shape[1]) + NORM_EPS)


def _store_norm_slice(operands, lhs_ref, r_ref, slot, step):
    xg, r = operands
    n_slices = lhs_ref.shape[1] // NORM_ROWS
    start = pl.multiple_of(jnp.clip(step, 0, n_slices - 1) * NORM_ROWS, NORM_ROWS)
    rows = pl.ds(start, NORM_ROWS)
    lhs_ref[slot, rows, :] = xg
    r_ref[slot, rows, :] = r


def _norm_slice_spec(m, tm, d, first_step):
    nb = m // tm
    n_slices = tm // NORM_ROWS

    def index(i, j):
        s = jnp.clip(j - first_step, 0, n_slices - 1)
        return jnp.where(i < nb, i * n_slices + s, nb * n_slices - 1), 0

    return pl.BlockSpec((NORM_ROWS, d), index)


def _inproj_kernel(x_ref, g_ref, w_ref, ta_ref, tb_ref, bret_ref, proj_ref, gate_ref,
                   lhs_ref, r_ref, tab_ref):
    i = pl.program_id(0)
    j = pl.program_id(1)
    tm, d = lhs_ref.shape[1:]
    tn = w_ref.shape[1]
    t_ak = OFF_AK // tn
    t_rq = OFF_RQ // tn
    t_rk = OFF_RK // tn
    t_rv = OFF_RV // tn
    t_rg = OFF_RG // tn
    c_ref, s_ref, cos_ref, sin_ref = (tab_ref.at[:, s] for s in _chunks(tab_ref.shape[1]))
    mul_slot = (i + 1) % 2
    norm_slot = i % 2

    slice_step = j - t_rq

    @pl.when(i == 0)
    def _():
        _store_norm_slice(_norm_operands(x_ref, g_ref), lhs_ref, r_ref, norm_slot, slice_step)

    multiplying = i > 0

    @pl.when(multiplying & (j == 0))
    def _():
        rc_b, rs_b, rss_b, tc_b, ts_b = (tb_ref[:, s] for s in _chunks(tb_ref.shape[1]))
        for c in range(tm // LANES):
            rows = slice(c * LANES, (c + 1) * LANES)
            rc_a, rs_a, rss_a, tc_a, ts_a = (ta_ref[c:c + 1, s] for s in _chunks(ta_ref.shape[1]))
            c_ref[rows, :] = rc_a * rc_b - rs_a * rs_b
            s_ref[rows, :] = rss_a * rc_b + rc_a * rss_b
            cos_ref[rows, :] = tc_a * tc_b - ts_a * ts_b
            sin_ref[rows, :] = ts_a * tc_b + tc_a * ts_b

    def tile(ref, epilogue, norm_slice):
        if norm_slice:
            nxt = _norm_operands(x_ref, g_ref)
        acc = _dot(lhs_ref[mul_slot], w_ref[...].astype(BF16))
        r = r_ref[mul_slot]
        vals = epilogue([acc[:, s] * r for s in _chunks(tn)])
        for s, v in zip(_chunks(tn), vals):
            ref[:, s] = v.astype(ref.dtype)
        if norm_slice:
            _store_norm_slice(nxt, lhs_ref, r_ref, norm_slot, slice_step)

    def rope(x):
        lane = lax.broadcasted_iota(jnp.int32, x.shape, 1)
        first_half = (lane % ATT_HEAD_DIM) < ROPE_DIM // 2
        partner = jnp.where(first_half, pltpu.roll(x, LANES - ROPE_DIM // 2, 1), pltpu.roll(x, ROPE_DIM // 2, 1))
        return x * c_ref[...] + partner * s_ref[...]

    def rot_pairs(xs, scale):
        out = []
        for x1, x2 in zip(xs[0::2], xs[1::2]):
            cos, sin = cos_ref[...], sin_ref[...]
            out += [(x1 * cos - x2 * sin) * scale, (x2 * cos + x1 * sin) * scale]
        return out

    @pl.when(multiplying & (j < t_ak))
    def _():
        tile(proj_ref, lambda xs: [rope(x) * (ATT_HEAD_DIM ** -0.5 * LOG2E) for x in xs], False)

    @pl.when(multiplying & (j == t_ak))
    def _():
        nk = ATT_KV_WIDTH // LANES
        tile(proj_ref, lambda xs: [rope(x) for x in xs[:nk]] + xs[nk:], False)

    @pl.when(multiplying & (j >= t_rq) & (j < t_rk))
    def _():
        tile(proj_ref, lambda xs: rot_pairs(xs, 1.0), True)

    @pl.when(multiplying & (j >= t_rk) & (j < t_rv))
    def _():
        tile(proj_ref, lambda xs: rot_pairs(xs, RET_HEAD_DIM ** -0.5), True)

    @pl.when(multiplying & (j >= t_rv) & (j < t_rg))
    def _():
        tile(proj_ref, lambda xs: xs, True)

    @pl.when(multiplying & (j >= t_rg))
    def _():
        tile(gate_ref, lambda xs: [x * jax.nn.sigmoid(x) * bret_ref[:, s] for x, s in zip(xs, _chunks(tn))], True)


def _inproj(x, g, w, ta, tb, beta_ret):
    m, d = x.shape
    tm, tn = IN_TM, IN_TN
    assert 2 * ATT_KV_WIDTH == tn
    assert tb.shape[0] == LANES and ta.shape[0] * LANES == m
    nb = m // tm
    n_tiles = IN_COLS // tn
    n_slices = tm // NORM_ROWS
    t_ak = OFF_AK // tn
    t_rq = OFF_RQ // tn
    t_rg = OFF_RG // tn
    assert t_rq + n_slices <= n_tiles

    def proj_tile(i, j):
        moved = jnp.minimum(j, t_rg - 1) - 1
        tile = jnp.where(j < t_ak, j, jnp.where(j == t_ak, POS_KV // tn, moved))
        return jnp.maximum(i - 1, 0), jnp.where(i == 0, 0, tile)

    def gate_tile(i, j):
        return jnp.maximum(i - 1, 0), jnp.where(i == 0, 0, jnp.maximum(j - t_rg, 0))

    return pl.pallas_call(
        _inproj_kernel,
        out_shape=[jax.ShapeDtypeStruct((m, PROJ_COLS), BF16), jax.ShapeDtypeStruct((m, RET_WIDTH), F32)],
        grid=(nb + 1, n_tiles),
        in_specs=[
            _norm_slice_spec(m, tm, d, t_rq),
            pl.BlockSpec((1, d), lambda i, j: (0, 0)),
            pl.BlockSpec((d, tn), lambda i, j: (0, jnp.where(i == 0, 0, j))),
            pl.BlockSpec((tm // LANES, ta.shape[1]), lambda i, j: (jnp.maximum(i - 1, 0), 0)),
            pl.BlockSpec(tb.shape, lambda i, j: (0, 0)),
            pl.BlockSpec((1, tn), lambda i, j: (0, jnp.maximum(j - t_rg, 0))),
        ],
        out_specs=[pl.BlockSpec((tm, tn), proj_tile), pl.BlockSpec((tm, tn), gate_tile)],
        scratch_shapes=[
            pltpu.VMEM((2, tm, d), BF16),
            pltpu.VMEM((2, tm, LANES), F32),
            pltpu.VMEM((tm, 4 * LANES), F32),
        ],
        compiler_params=_params(("arbitrary", "arbitrary")),
        name="in_proj",
    )(x, g, w, ta, tb, beta_ret)


def _attn_items(first_block, sink_ref, q_ref, kvc_ref, kvp_ref, beta_ref, o_ref):
    blk = ATT_BLOCK
    hd = ATT_HEAD_DIM
    jj = lax.broadcasted_iota(jnp.int32, (blk, blk), 0)
    ii = lax.broadcasted_iota(jnp.int32, (blk, blk), 1)
    upper = jj > ii
    lo_lanes = lax.broadcasted_iota(jnp.int32, (2 * blk, LANES), 1) < hd
    prev_bias = None if first_block is None else jnp.where(first_block, -jnp.inf, 0.0)

    v_all = jnp.concatenate([kvp_ref[:, ATT_KV_WIDTH:], kvc_ref[:, ATT_KV_WIDTH:]], axis=0)
    v_t = v_all.astype(F32).T.astype(BF16)

    keys = []
    for kc in range(ATT_KV_WIDTH // LANES):
        s = slice(kc * LANES, (kc + 1) * LANES)
        kcol = jnp.concatenate([kvp_ref[:, s], kvc_ref[:, s]], axis=0).astype(F32)
        kswap = pltpu.roll(kcol, hd, 1)
        for hh in range(2):
            own, other = (kcol, kswap) if hh == 0 else (kswap, kcol)
            keys.append((jnp.where(lo_lanes, own, 0.0).astype(BF16),
                         jnp.where(lo_lanes, 0.0, other).astype(BF16)))

    n_cols = ATT_WIDTH // LANES
    cols_per_kv = n_cols // ATT_KV_HEADS

    def scores(col):
        qc = q_ref[:, col * LANES:(col + 1) * LANES]
        return [_dot_t(kk, qc) for kk in keys[col // cols_per_kv]]

    pending = [scores(c) for c in range(ATT_LOOKAHEAD)]

    def item(col):
        sts = pending.pop(0)
        if col + ATT_LOOKAHEAD < n_cols:
            pending.append(scores(col + ATT_LOOKAHEAD))
        kvh = col // cols_per_kv
        cs = slice(col * LANES, (col + 1) * LANES)
        v_h = v_t[kvh * hd:(kvh + 1) * hd, :]
        outs = []
        for half, st in enumerate(sts):
            head = 2 * col + half
            prev = st[:blk] if prev_bias is None else st[:blk] + prev_bias
            comb = jnp.where(upper, prev, st[blk:])
            sink = sink_ref[head] * LOG2E
            m = jnp.maximum(jnp.max(comb, axis=0, keepdims=True), sink)
            p = jnp.exp2(comb - m)
            denom = jnp.sum(p, axis=0, keepdims=True) + jnp.exp2(sink - m)
            pp = jnp.concatenate([jnp.where(upper, p, 0.0), jnp.where(upper, 0.0, p)], axis=0)
            ot = _dot(v_h, pp.astype(BF16))
            outs.append(ot * (1.0 / denom))
        o_pair = jnp.concatenate(outs, axis=0).T
        o_ref[:, cs] = (o_pair * beta_ref[:, cs]).astype(o_ref.dtype)

    return [functools.partial(item, col) for col in range(n_cols)]


def _ret_items(q_ref, k_ref, v_ref, gate_ref, o_ref, state_ref, dmask_ref, xi_ref, zeta_ref, decay_ref):
    dh = RET_HEAD_DIM

    def first_products(h):
        hs = slice(h * dh, (h + 1) * dh)
        q = q_ref[:, hs]
        state = state_ref[h]
        return _dot_t(q, k_ref[:, hs]), _dot(q, state.astype(BF16)), state

    pending = [first_products(h) for h in range(RET_LOOKAHEAD)]

    def item(h):
        hs = slice(h * dh, (h + 1) * dh)
        qk, cross, state = pending.pop(0)
        if h + RET_LOOKAHEAD < RET_HEADS:
            pending.append(first_products(h + RET_LOOKAHEAD))
        k = k_ref[:, hs]
        v = v_ref[:, hs]
        xi = xi_ref[h]
        zeta = zeta_ref[h]

        a = qk * dmask_ref[h]
        inner = _dot(a.astype(BF16), v)
        o = inner + cross * jnp.concatenate([xi, xi], axis=1)

        mu = jnp.mean(o, axis=-1, keepdims=True)
        d = o - mu
        var = jnp.mean(d * d, axis=-1, keepdims=True)
        o = d * lax.rsqrt(var + RET_EPS)
        o_ref[:, hs] = (o * gate_ref[:, hs]).astype(o_ref.dtype)

        kz = (k.astype(F32) * jnp.concatenate([zeta, zeta], axis=1)).astype(BF16)
        kv = lax.dot_general(kz, v, (((0,), (0,)), ((), ())), preferred_element_type=F32)
        state_ref[h] = state * decay_ref[h][0:1, :] + kv

    return [functools.partial(item, h) for h in range(RET_HEADS)]


def _mixer_kernel(sink_ref, lg_ref, aq_ref, kv_ref, kvp_ref, rq_ref, rk_ref, rv_ref, gate_ref,
                  batt_ref, o_ref, state_ref, dmask_ref, xi_ref, zeta_ref, decay_ref):
    n = pl.program_id(0)
    ch = RET_CHUNK
    dh = RET_HEAD_DIM

    @pl.when(n == 0)
    def _():
        state_ref[...] = jnp.zeros_like(state_ref)
        ri = lax.broadcasted_iota(jnp.int32, (ch, ch), 0)
        ci = lax.broadcasted_iota(jnp.int32, (ch, ch), 1)
        diff = (ri - ci).astype(F32)
        row = ri.astype(F32)
        for h in range(RET_HEADS):
            lg = lg_ref[h]
            dmask_ref[h] = jnp.where(diff >= 0, jnp.exp(lg * jnp.maximum(diff, 0.0)), 0.0)
            xi_ref[h] = jnp.exp(lg * (row + 1.0))
            zeta_ref[h] = jnp.exp(lg * ((ch - 1.0) - row))
            decay_ref[h] = jnp.exp(jnp.full((SUBLANES, dh), lg * ch, F32))

    att_out = o_ref.at[:, :ATT_WIDTH]
    ret_out = o_ref.at[:, ATT_WIDTH:]
    for b in range(aq_ref.shape[0] // ch):
        rows = slice(b * ch, (b + 1) * ch)
        prev = kvp_ref if b == 0 else kv_ref.at[slice((b - 1) * ch, b * ch)]
        att = _attn_items(n == 0 if b == 0 else None, sink_ref, aq_ref.at[rows], kv_ref.at[rows], prev,
                          batt_ref, att_out.at[rows])
        ret = _ret_items(rq_ref.at[rows], rk_ref.at[rows], rv_ref.at[rows], gate_ref.at[rows],
                         ret_out.at[rows], state_ref, dmask_ref, xi_ref, zeta_ref, decay_ref)
        per_head = len(att) // len(ret)
        for h, ret_item in enumerate(ret):
            for att_item in att[h * per_head:(h + 1) * per_head]:
                att_item()
            ret_item()


def _mixer(proj, gate, sinks, log_g, beta_att):
    s = proj.shape[0]
    rows, ch, dh = MIX_ROWS, RET_CHUNK, RET_HEAD_DIM
    kv_w = 2 * ATT_KV_WIDTH
    wide = lambda pos: pl.BlockSpec((rows, RET_WIDTH), lambda n: (n, pos // RET_WIDTH))
    return pl.pallas_call(
        _mixer_kernel,
        out_shape=jax.ShapeDtypeStruct((s, ATT_WIDTH + RET_WIDTH), BF16),
        grid=(s // rows,),
        in_specs=[
            pl.BlockSpec(memory_space=pltpu.SMEM),
            pl.BlockSpec(memory_space=pltpu.SMEM),
            pl.BlockSpec((rows, ATT_WIDTH), lambda n: (n, POS_AQ // ATT_WIDTH)),
            pl.BlockSpec((rows, kv_w), lambda n: (n, POS_KV // kv_w)),
            pl.BlockSpec((ch, kv_w), lambda n: (jnp.maximum(n * (rows // ch) - 1, 0), POS_KV // kv_w)),
            wide(POS_RQ), wide(POS_RK), wide(POS_RV),
            pl.BlockSpec((rows, RET_WIDTH), lambda n: (n, 0)),
            pl.BlockSpec((1, ATT_WIDTH), lambda n: (0, 0)),
        ],
        out_specs=pl.BlockSpec((rows, ATT_WIDTH + RET_WIDTH), lambda n: (n, 0)),
        scratch_shapes=[
            pltpu.VMEM((RET_HEADS, dh, dh), F32),
            pltpu.VMEM((RET_HEADS, ch, ch), F32),
            pltpu.VMEM((RET_HEADS, ch, ch), F32),
            pltpu.VMEM((RET_HEADS, ch, ch), F32),
            pltpu.VMEM((RET_HEADS, SUBLANES, dh), F32),
        ],
        compiler_params=_params(("arbitrary",)),
        name="token_mixer",
    )(sinks, log_g, proj, proj, proj, proj, proj, proj, gate, beta_att)


def _outproj_kernel(x_ref, a_ref, w_ref, g_ref, o_ref, xg_ref, r_ref, *, d_model):
    j = pl.program_id(1)
    x2 = x_ref[...] + _dot(a_ref[...], w_ref[...].astype(BF16))
    o_ref[...] = x2
    xg_ref[...] = (x2 * g_ref[...]).astype(xg_ref.dtype)
    ss = _row_sumsq(x2)

    @pl.when(j == 0)
    def _():
        r_ref[...] = ss

    @pl.when(j > 0)
    def _():
        r_ref[...] += ss

    @pl.when(j == pl.num_programs(1) - 1)
    def _():
        r_ref[...] = lax.rsqrt(r_ref[...] * (1.0 / d_model) + NORM_EPS)


def _outproj(x, a, w, g):
    m, d = x.shape
    kdim = a.shape[1]
    tm, tn = OUT_TM, OUT_TN
    tile = lambda i, j: (i, j)
    row = lambda i, j: (i, 0)
    return pl.pallas_call(
        functools.partial(_outproj_kernel, d_model=d),
        out_shape=[jax.ShapeDtypeStruct((m, d), F32), jax.ShapeDtypeStruct((m, d), BF16),
                   jax.ShapeDtypeStruct((m, LANES), F32)],
        grid=(m // tm, d // tn),
        in_specs=[
            pl.BlockSpec((tm, tn), tile),
            pl.BlockSpec((tm, kdim), row),
            pl.BlockSpec((kdim, tn), lambda i, j: (0, j)),
            pl.BlockSpec((1, tn), lambda i, j: (0, j)),
        ],
        out_specs=[pl.BlockSpec((tm, tn), tile), pl.BlockSpec((tm, tn), tile), pl.BlockSpec((tm, LANES), row)],
        compiler_params=_params(("arbitrary", "arbitrary")),
        name="out_proj",
    )(x, a, w, g)


def _up_kernel(a_ref, r_ref, w_ref, o_ref):
    acc = _dot(a_ref[...], w_ref[...].astype(BF16))
    r = r_ref[...]
    for s in _chunks(o_ref.shape[1]):
        o_ref[:, s] = jnp.square(jnp.maximum(acc[:, s] * r, 0.0)).astype(o_ref.dtype)


def _up(xg, r, w):
    m, d = xg.shape
    f = w.shape[1]
    tm, tn = UP_TM, UP_TN
    return pl.pallas_call(
        _up_kernel,
        out_shape=jax.ShapeDtypeStruct((m, f), BF16),
        grid=(m // tm, f // tn),
        in_specs=[
            pl.BlockSpec((tm, d), lambda i, j: (i, 0)),
            pl.BlockSpec((tm, LANES), lambda i, j: (i, 0)),
            pl.BlockSpec((d, tn), lambda i, j: (0, j)),
        ],
        out_specs=pl.BlockSpec((tm, tn), lambda i, j: (i, j)),
        compiler_params=_params(("arbitrary", "arbitrary")),
        name="mlp_up",
    )(xg, r, w)


def _down_kernel(x_ref, u_ref, w_ref, o_ref):
    k = pl.program_id(2)

    @pl.when(k == 0)
    def _():
        o_ref[...] = x_ref[...]

    o_ref[...] += _dot(u_ref[...], w_ref[...].astype(BF16))


def _down(x, u, w):
    m, d = x.shape
    f = u.shape[1]
    tm, tn, tk = DOWN_TM, DOWN_TN, DOWN_TK
    return pl.pallas_call(
        _down_kernel,
        out_shape=jax.ShapeDtypeStruct((m, d), F32),
        grid=(m // tm, d // tn, f // tk),
        in_specs=[
            pl.BlockSpec((tm, tn), lambda i, j, k: (i, j)),
            pl.BlockSpec((tm, tk), lambda i, j, k: (i, k)),
            pl.BlockSpec((tk, tn), lambda i, j, k: (k, j)),
        ],
        out_specs=pl.BlockSpec((tm, tn), lambda i, j, k: (i, j)),
        compiler_params=_params(("arbitrary", "arbitrary", "arbitrary")),
        name="mlp_down",
    )(x, u, w)


def _rmsnorm_kernel(x_ref, g_ref, o_ref):
    x = x_ref[...]
    r = lax.rsqrt(_row_sumsq(x) * (1.0 / x.shape[1]) + NORM_EPS)
    for s in _chunks(o_ref.shape[1]):
        o_ref[:, s] = x[:, s] * r * g_ref[:, s]


def _rmsnorm(x, g):
    m, d = x.shape
    tm = ROW_TM
    return pl.pallas_call(
        _rmsnorm_kernel,
        out_shape=jax.ShapeDtypeStruct((m, d), F32),
        grid=(m // tm,),
        in_specs=[pl.BlockSpec((tm, d), lambda i: (i, 0)), pl.BlockSpec((1, d), lambda i: (0, 0))],
        out_specs=pl.BlockSpec((tm, d), lambda i: (i, 0)),
        compiler_params=_params(("arbitrary",)),
        name="final_rmsnorm",
    )(x, g)


def _base_tables(pos):
    n = pos.shape[0]
    rest = ATT_HEAD_DIM - ROPE_DIM
    rope_freq = ROPE_THETA ** (-jnp.arange(0, ROPE_DIM, 2, dtype=F32) / ROPE_DIM)
    ang = pos[:, None] * rope_freq[None, :]
    cos, sin = jnp.cos(ang), jnp.sin(ang)
    ones, zeros = jnp.ones((n, rest), F32), jnp.zeros((n, rest), F32)
    rep = LANES // ATT_HEAD_DIM
    rope = [jnp.tile(jnp.concatenate(parts, axis=1), (1, rep)) for parts in
            ([cos, cos, ones], [sin, sin, zeros], [-sin, sin, zeros])]
    ret_freq = 1.0 / (RET_THETA ** jnp.linspace(0.0, 1.0, RET_HEAD_DIM // 2, dtype=F32))
    ang = pos[:, None] * ret_freq[None, :]
    return jnp.concatenate(rope + [jnp.cos(ang), jnp.sin(ang)], axis=1)


def kernel(x, g_mix, w_in, att_sinks, beta_att, beta_ret, w_out, g_mlp, w_up, w_down, g_final):
    b, s, d = x.shape
    depth = w_in.shape[0]
    ta = _base_tables(jnp.arange(s // LANES, dtype=F32) * LANES)
    tb = _base_tables(jnp.arange(LANES, dtype=F32))
    log_g = jnp.log1p(-(2.0 ** (-5.0 - jnp.arange(RET_HEADS, dtype=F32))))

    outs = []
    for bi in range(b):
        xb = x[bi]
        for l in range(depth):
            proj, gate = _inproj(xb, g_mix[l][None, :], w_in[l], ta, tb, beta_ret[l][None, :])
            mixed = _mixer(proj, gate, att_sinks[l], log_g, beta_att[l][None, :])
            xb, xg, r = _outproj(xb, mixed, w_out[l], g_mlp[l][None, :])
            u = _up(xg, r, w_up[l])
            xb = _down(xb, u, w_down[l])
        outs.append(_rmsnorm(xb, g_final[None, :]))
    return outs[0][None] if b == 1 else jnp.stack(outs, axis=0)
```

```python
import functools
import math

import jax
import jax.numpy as jnp
from jax import lax
from jax.experimental import pallas as pl
from jax.experimental.pallas import tpu as pltpu

F32 = jnp.float32
BF16 = jnp.bfloat16

ATT_HEADS = 32
ATT_KV_HEADS = 4
ATT_HEAD_DIM = 64
ATT_GROUP = ATT_HEADS // ATT_KV_HEADS
ATT_WIDTH = ATT_HEADS * ATT_HEAD_DIM
ATT_KV_WIDTH = ATT_KV_HEADS * ATT_HEAD_DIM
ATT_BLOCK = 128
ROPE_DIM = ATT_HEAD_DIM // 4
ROPE_THETA = 500000.0
RET_HEADS = 8
RET_HEAD_DIM = 256
RET_WIDTH = RET_HEADS * RET_HEAD_DIM
RET_CHUNK = 128
RET_THETA = 10000.0
RET_EPS = 1e-6
NORM_EPS = 1e-5
LOG2E = math.log2(math.e)

LANES = 128
SUBLANES = 8
VMEM_LIMIT_BYTES = 58 * 1024 * 1024

OFF_AQ = 0
OFF_AK = OFF_AQ + ATT_WIDTH
OFF_AV = OFF_AK + ATT_KV_WIDTH
OFF_RQ = OFF_AV + ATT_KV_WIDTH
OFF_RK = OFF_RQ + RET_WIDTH
OFF_RV = OFF_RK + RET_WIDTH
OFF_RG = OFF_RV + RET_WIDTH
IN_COLS = OFF_RG + RET_WIDTH

POS_AQ = 0
POS_RQ = POS_AQ + ATT_WIDTH
POS_RK = POS_RQ + RET_WIDTH
POS_RV = POS_RK + RET_WIDTH
POS_KV = POS_RV + RET_WIDTH
PROJ_COLS = POS_KV + 2 * ATT_KV_WIDTH

IN_TM, IN_TN = 1024, 512
NORM_ROWS = 64
OUT_TM, OUT_TN = 1024, 512
OUT_W_SLOTS = 3
UP_TM, UP_TN = 1024, 512
DOWN_TM, DOWN_TN, DOWN_TK = 2048, 1024, 1024
ROW_TM = 512
ATT_LOOKAHEAD = 3
RET_LOOKAHEAD = 2
MIX_ROWS = 512


def _params(semantics):
    return pltpu.CompilerParams(dimension_semantics=semantics, vmem_limit_bytes=VMEM_LIMIT_BYTES)


def _dot(a, b):
    return jnp.dot(a, b, preferred_element_type=F32)


def _dot_t(a, b):
    return lax.dot_general(a, b, (((1,), (1,)), ((), ())), preferred_element_type=F32)


def _row_sumsq(x):
    return jnp.broadcast_to(jnp.sum(x * x, axis=-1, keepdims=True), (x.shape[0], LANES))


def _chunks(width):
    return [slice(c * LANES, (c + 1) * LANES) for c in range(width // LANES)]


def _norm_operands(x_ref, g_ref):
    x = x_ref[...]
    return (x * g_ref[...]).astype(BF16), lax.rsqrt(_row_sumsq(x) * (1.0 / x.shape[1]) + NORM_EPS)


def _store_norm_slice(operands, lhs_ref, r_ref, slot, step):
    xg, r = operands
    n_slices = lhs_ref.shape[1] // NORM_ROWS
    start = pl.multiple_of(jnp.minimum(step, n_slices - 1) * NORM_ROWS, NORM_ROWS)
    rows = pl.ds(start, NORM_ROWS)
    lhs_ref[slot, rows, :] = xg
    r_ref[slot, rows, :] = r


def _norm_slice_spec(m, tm, d):
    nb = m // tm
    n_slices = tm // NORM_ROWS

    def index(i, j):
        return jnp.where(i < nb, i * n_slices + jnp.minimum(j, n_slices - 1), nb * n_slices - 1), 0

    return pl.BlockSpec((NORM_ROWS, d), index)


def _inproj_kernel(x_ref, g_ref, w_ref, ta_ref, tb_ref, bret_ref, proj_ref, gate_ref,
                   lhs_ref, r_ref, tab_ref):
    i = pl.program_id(0)
    j = pl.program_id(1)
    tm, d = lhs_ref.shape[1:]
    tn = w_ref.shape[1]
    t_ak = OFF_AK // tn
    t_rq = OFF_RQ // tn
    t_rk = OFF_RK // tn
    t_rv = OFF_RV // tn
    t_rg = OFF_RG // tn
    c_ref, s_ref, cos_ref, sin_ref = (tab_ref.at[:, s] for s in _chunks(tab_ref.shape[1]))
    mul_slot = (i + 1) % 2
    norm_slot = i % 2

    @pl.when(i == 0)
    def _():
        _store_norm_slice(_norm_operands(x_ref, g_ref), lhs_ref, r_ref, norm_slot, j)

    multiplying = i > 0

    @pl.when(multiplying & (j == 0))
    def _():
        rc_b, rs_b, rss_b, tc_b, ts_b = (tb_ref[:, s] for s in _chunks(tb_ref.shape[1]))
        for c in range(tm // LANES):
            rows = slice(c * LANES, (c + 1) * LANES)
            rc_a, rs_a, rss_a, tc_a, ts_a = (ta_ref[c:c + 1, s] for s in _chunks(ta_ref.shape[1]))
            c_ref[rows, :] = rc_a * rc_b - rs_a * rs_b
            s_ref[rows, :] = rss_a * rc_b + rc_a * rss_b
            cos_ref[rows, :] = tc_a * tc_b - ts_a * ts_b
            sin_ref[rows, :] = ts_a * tc_b + tc_a * ts_b

    def tile(ref, epilogue):
        nxt = _norm_operands(x_ref, g_ref)
        acc = _dot(lhs_ref[mul_slot], w_ref[...].astype(BF16))
        r = r_ref[mul_slot]
        vals = epilogue([acc[:, s] * r for s in _chunks(tn)])
        for s, v in zip(_chunks(tn), vals):
            ref[:, s] = v.astype(ref.dtype)
        _store_norm_slice(nxt, lhs_ref, r_ref, norm_slot, j)

    def rope(x):
        lane = lax.broadcasted_iota(jnp.int32, x.shape, 1)
        first_half = (lane % ATT_HEAD_DIM) < ROPE_DIM // 2
        partner = jnp.where(first_half, pltpu.roll(x, LANES - ROPE_DIM // 2, 1), pltpu.roll(x, ROPE_DIM // 2, 1))
        return x * c_ref[...] + partner * s_ref[...]

    def rot_pairs(xs, scale):
        out = []
        for x1, x2 in zip(xs[0::2], xs[1::2]):
            cos, sin = cos_ref[...], sin_ref[...]
            out += [(x1 * cos - x2 * sin) * scale, (x2 * cos + x1 * sin) * scale]
        return out

    @pl.when(multiplying & (j < t_ak))
    def _():
        tile(proj_ref, lambda xs: [rope(x) * (ATT_HEAD_DIM ** -0.5 * LOG2E) for x in xs])

    @pl.when(multiplying & (j == t_ak))
    def _():
        nk = ATT_KV_WIDTH // LANES
        tile(proj_ref, lambda xs: [rope(x) for x in xs[:nk]] + xs[nk:])

    @pl.when(multiplying & (j >= t_rq) & (j < t_rk))
    def _():
        tile(proj_ref, lambda xs: rot_pairs(xs, 1.0))

    @pl.when(multiplying & (j >= t_rk) & (j < t_rv))
    def _():
        tile(proj_ref, lambda xs: rot_pairs(xs, RET_HEAD_DIM ** -0.5))

    @pl.when(multiplying & (j >= t_rv) & (j < t_rg))
    def _():
        tile(proj_ref, lambda xs: xs)

    @pl.when(multiplying & (j >= t_rg))
    def _():
        tile(gate_ref, lambda xs: [x * jax.nn.sigmoid(x) * bret_ref[:, s] for x, s in zip(xs, _chunks(tn))])


def _inproj(x, g, w, ta, tb, beta_ret):
    m, d = x.shape
    tm, tn = IN_TM, IN_TN
    assert 2 * ATT_KV_WIDTH == tn
    assert tb.shape[0] == LANES and ta.shape[0] * LANES == m
    nb = m // tm
    n_tiles = IN_COLS // tn
    n_slices = tm // NORM_ROWS
    assert n_slices <= n_tiles
    t_ak = OFF_AK // tn
    t_rg = OFF_RG // tn

    def proj_tile(i, j):
        moved = jnp.minimum(j, t_rg - 1) - 1
        tile = jnp.where(j < t_ak, j, jnp.where(j == t_ak, POS_KV // tn, moved))
        return jnp.maximum(i - 1, 0), jnp.where(i == 0, 0, tile)

    def gate_tile(i, j):
        return jnp.maximum(i - 1, 0), jnp.where(i == 0, 0, jnp.maximum(j - t_rg, 0))

    return pl.pallas_call(
        _inproj_kernel,
        out_shape=[jax.ShapeDtypeStruct((m, PROJ_COLS), BF16), jax.ShapeDtypeStruct((m, RET_WIDTH), F32)],
        grid=(nb + 1, n_tiles),
        in_specs=[
            _norm_slice_spec(m, tm, d),
            pl.BlockSpec((1, d), lambda i, j: (0, 0)),
            pl.BlockSpec((d, tn), lambda i, j: (0, jnp.where(i == 0, 0, j))),
            pl.BlockSpec((tm // LANES, ta.shape[1]), lambda i, j: (jnp.maximum(i - 1, 0), 0)),
            pl.BlockSpec(tb.shape, lambda i, j: (0, 0)),
            pl.BlockSpec((1, tn), lambda i, j: (0, jnp.maximum(j - t_rg, 0))),
        ],
        out_specs=[pl.BlockSpec((tm, tn), proj_tile), pl.BlockSpec((tm, tn), gate_tile)],
        scratch_shapes=[
            pltpu.VMEM((2, tm, d), BF16),
            pltpu.VMEM((2, tm, LANES), F32),
            pltpu.VMEM((tm, 4 * LANES), F32),
        ],
        compiler_params=_params(("arbitrary", "arbitrary")),
        name="in_proj",
    )(x, g, w, ta, tb, beta_ret)


def _attn_items(first_block, sink_ref, q_ref, kvc_ref, kvp_ref, beta_ref, o_ref):
    blk = ATT_BLOCK
    hd = ATT_HEAD_DIM
    jj = lax.broadcasted_iota(jnp.int32, (blk, blk), 0)
    ii = lax.broadcasted_iota(jnp.int32, (blk, blk), 1)
    upper = jj > ii
    lo_lanes = lax.broadcasted_iota(jnp.int32, (2 * blk, LANES), 1) < hd
    prev_bias = None if first_block is None else jnp.where(first_block, -jnp.inf, 0.0)

    v_all = jnp.concatenate([kvp_ref[:, ATT_KV_WIDTH:], kvc_ref[:, ATT_KV_WIDTH:]], axis=0)
    v_t = v_all.astype(F32).T.astype(BF16)

    keys = []
    for kc in range(ATT_KV_WIDTH // LANES):
        s = slice(kc * LANES, (kc + 1) * LANES)
        kcol = jnp.concatenate([kvp_ref[:, s], kvc_ref[:, s]], axis=0).astype(F32)
        kswap = pltpu.roll(kcol, hd, 1)
        for hh in range(2):
            own, other = (kcol, kswap) if hh == 0 else (kswap, kcol)
            keys.append((jnp.where(lo_lanes, own, 0.0).astype(BF16),
                         jnp.where(lo_lanes, 0.0, other).astype(BF16)))

    n_cols = ATT_WIDTH // LANES
    cols_per_kv = n_cols // ATT_KV_HEADS

    def scores(col):
        qc = q_ref[:, col * LANES:(col + 1) * LANES]
        return [_dot_t(kk, qc) for kk in keys[col // cols_per_kv]]

    pending = [scores(c) for c in range(ATT_LOOKAHEAD)]

    def item(col):
        sts = pending.pop(0)
        if col + ATT_LOOKAHEAD < n_cols:
            pending.append(scores(col + ATT_LOOKAHEAD))
        kvh = col // cols_per_kv
        cs = slice(col * LANES, (col + 1) * LANES)
        v_h = v_t[kvh * hd:(kvh + 1) * hd, :]
        outs = []
        for half, st in enumerate(sts):
            head = 2 * col + half
            prev = st[:blk] if prev_bias is None else st[:blk] + prev_bias
            comb = jnp.where(upper, prev, st[blk:])
            sink = sink_ref[head] * LOG2E
            m = jnp.maximum(jnp.max(comb, axis=0, keepdims=True), sink)
            p = jnp.exp2(comb - m)
            denom = jnp.sum(p, axis=0, keepdims=True) + jnp.exp2(sink - m)
            pp = jnp.concatenate([jnp.where(upper, p, 0.0), jnp.where(upper, 0.0, p)], axis=0)
            ot = _dot(v_h, pp.astype(BF16))
            outs.append(ot * (1.0 / denom))
        o_pair = jnp.concatenate(outs, axis=0).T
        o_ref[:, cs] = (o_pair * beta_ref[:, cs]).astype(o_ref.dtype)

    return [functools.partial(item, col) for col in range(n_cols)]


def _ret_items(q_ref, k_ref, v_ref, gate_ref, o_ref, state_ref, dmask_ref, xi_ref, zeta_ref, decay_ref):
    dh = RET_HEAD_DIM

    def first_products(h):
        hs = slice(h * dh, (h + 1) * dh)
        q = q_ref[:, hs]
        state = state_ref[h]
        return _dot_t(q, k_ref[:, hs]), _dot(q, state.astype(BF16)), state

    pending = [first_products(h) for h in range(RET_LOOKAHEAD)]

    def item(h):
        hs = slice(h * dh, (h + 1) * dh)
        qk, cross, state = pending.pop(0)
        if h + RET_LOOKAHEAD < RET_HEADS:
            pending.append(first_products(h + RET_LOOKAHEAD))
        k = k_ref[:, hs]
        v = v_ref[:, hs]
        xi = xi_ref[h]
        zeta = zeta_ref[h]

        a = qk * dmask_ref[h]
        inner = _dot(a.astype(BF16), v)
        o = inner + cross * jnp.concatenate([xi, xi], axis=1)

        mu = jnp.mean(o, axis=-1, keepdims=True)
        d = o - mu
        var = jnp.mean(d * d, axis=-1, keepdims=True)
        o = d * lax.rsqrt(var + RET_EPS)
        o_ref[:, hs] = (o * gate_ref[:, hs]).astype(o_ref.dtype)

        kz = (k.astype(F32) * jnp.concatenate([zeta, zeta], axis=1)).astype(BF16)
        kv = lax.dot_general(kz, v, (((0,), (0,)), ((), ())), preferred_element_type=F32)
        state_ref[h] = state * decay_ref[h][0:1, :] + kv

    return [functools.partial(item, h) for h in range(RET_HEADS)]


def _mixer_kernel(sink_ref, lg_ref, aq_ref, kv_ref, kvp_ref, rq_ref, rk_ref, rv_ref, gate_ref,
                  batt_ref, o_ref, state_ref, dmask_ref, xi_ref, zeta_ref, decay_ref):
    n = pl.program_id(0)
    ch = RET_CHUNK
    dh = RET_HEAD_DIM

    @pl.when(n == 0)
    def _():
        state_ref[...] = jnp.zeros_like(state_ref)
        ri = lax.broadcasted_iota(jnp.int32, (ch, ch), 0)
        ci = lax.broadcasted_iota(jnp.int32, (ch, ch), 1)
        diff = (ri - ci).astype(F32)
        row = ri.astype(F32)
        for h in range(RET_HEADS):
            lg = lg_ref[h]
            dmask_ref[h] = jnp.where(diff >= 0, jnp.exp(lg * jnp.maximum(diff, 0.0)), 0.0)
            xi_ref[h] = jnp.exp(lg * (row + 1.0))
            zeta_ref[h] = jnp.exp(lg * ((ch - 1.0) - row))
            decay_ref[h] = jnp.exp(jnp.full((SUBLANES, dh), lg * ch, F32))

    att_out = o_ref.at[:, :ATT_WIDTH]
    ret_out = o_ref.at[:, ATT_WIDTH:]
    for b in range(aq_ref.shape[0] // ch):
        rows = slice(b * ch, (b + 1) * ch)
        prev = kvp_ref if b == 0 else kv_ref.at[slice((b - 1) * ch, b * ch)]
        att = _attn_items(n == 0 if b == 0 else None, sink_ref, aq_ref.at[rows], kv_ref.at[rows], prev,
                          batt_ref, att_out.at[rows])
        ret = _ret_items(rq_ref.at[rows], rk_ref.at[rows], rv_ref.at[rows], gate_ref.at[rows],
                         ret_out.at[rows], state_ref, dmask_ref, xi_ref, zeta_ref, decay_ref)
        per_head = len(att) // len(ret)
        for h, ret_item in enumerate(ret):
            for att_item in att[h * per_head:(h + 1) * per_head]:
                att_item()
            ret_item()


def _mixer(proj, gate, sinks, log_g, beta_att):
    s = proj.shape[0]
    rows, ch, dh = MIX_ROWS, RET_CHUNK, RET_HEAD_DIM
    kv_w = 2 * ATT_KV_WIDTH
    wide = lambda pos: pl.BlockSpec((rows, RET_WIDTH), lambda n: (n, pos // RET_WIDTH))
    return pl.pallas_call(
        _mixer_kernel,
        out_shape=jax.ShapeDtypeStruct((s, ATT_WIDTH + RET_WIDTH), BF16),
        grid=(s // rows,),
        in_specs=[
            pl.BlockSpec(memory_space=pltpu.SMEM),
            pl.BlockSpec(memory_space=pltpu.SMEM),
            pl.BlockSpec((rows, ATT_WIDTH), lambda n: (n, POS_AQ // ATT_WIDTH)),
            pl.BlockSpec((rows, kv_w), lambda n: (n, POS_KV // kv_w)),
            pl.BlockSpec((ch, kv_w), lambda n: (jnp.maximum(n * (rows // ch) - 1, 0), POS_KV // kv_w)),
            wide(POS_RQ), wide(POS_RK), wide(POS_RV),
            pl.BlockSpec((rows, RET_WIDTH), lambda n: (n, 0)),
            pl.BlockSpec((1, ATT_WIDTH), lambda n: (0, 0)),
        ],
        out_specs=pl.BlockSpec((rows, ATT_WIDTH + RET_WIDTH), lambda n: (n, 0)),
        scratch_shapes=[
            pltpu.VMEM((RET_HEADS, dh, dh), F32),
            pltpu.VMEM((RET_HEADS, ch, ch), F32),
            pltpu.VMEM((RET_HEADS, ch, ch), F32),
            pltpu.VMEM((RET_HEADS, ch, ch), F32),
            pltpu.VMEM((RET_HEADS, SUBLANES, dh), F32),
        ],
        compiler_params=_params(("arbitrary",)),
        name="token_mixer",
    )(sinks, log_g, proj, proj, proj, proj, proj, proj, gate, beta_att)


def _outproj_kernel(x_ref, a_ref, w_hbm, g_ref, o_ref, xg_ref, r_ref, w_buf, w_sem, *, d_model):
    i = pl.program_id(0)
    j = pl.program_id(1)
    n_tiles = pl.num_programs(1)
    n_steps = pl.num_programs(0) * n_tiles
    step = i * n_tiles + j
    tn = w_buf.shape[2]

    def w_copy(s):
        col = pl.multiple_of(lax.rem(s, n_tiles) * tn, tn)
        slot = lax.rem(s, OUT_W_SLOTS)
        return pltpu.make_async_copy(w_hbm.at[:, pl.ds(col, tn)], w_buf.at[slot], w_sem.at[slot])

    @pl.when(step == 0)
    def _():
        for s in range(OUT_W_SLOTS - 1):
            w_copy(s).start()

    @pl.when(step + (OUT_W_SLOTS - 1) < n_steps)
    def _():
        w_copy(step + (OUT_W_SLOTS - 1)).start()

    w_copy(step).wait()
    w = w_buf[lax.rem(step, OUT_W_SLOTS)].astype(BF16)
    x2 = x_ref[...] + _dot(a_ref[...], w)
    o_ref[...] = x2
    xg_ref[...] = (x2 * g_ref[...]).astype(xg_ref.dtype)
    ss = _row_sumsq(x2)

    @pl.when(j == 0)
    def _():
        r_ref[...] = ss

    @pl.when(j > 0)
    def _():
        r_ref[...] += ss

    @pl.when(j == pl.num_programs(1) - 1)
    def _():
        r_ref[...] = lax.rsqrt(r_ref[...] * (1.0 / d_model) + NORM_EPS)


def _outproj(x, a, w, g):
    m, d = x.shape
    kdim = a.shape[1]
    tm, tn = OUT_TM, OUT_TN
    tile = lambda i, j: (i, j)
    row = lambda i, j: (i, 0)
    assert (m // tm) * (d // tn) >= OUT_W_SLOTS - 1
    return pl.pallas_call(
        functools.partial(_outproj_kernel, d_model=d),
        out_shape=[jax.ShapeDtypeStruct((m, d), F32), jax.ShapeDtypeStruct((m, d), BF16),
                   jax.ShapeDtypeStruct((m, LANES), F32)],
        grid=(m // tm, d // tn),
        in_specs=[
            pl.BlockSpec((tm, tn), tile),
            pl.BlockSpec((tm, kdim), row),
            pl.BlockSpec(memory_space=pl.ANY),
            pl.BlockSpec((1, tn), lambda i, j: (0, j)),
        ],
        out_specs=[pl.BlockSpec((tm, tn), tile), pl.BlockSpec((tm, tn), tile), pl.BlockSpec((tm, LANES), row)],
        scratch_shapes=[pltpu.VMEM((OUT_W_SLOTS, kdim, tn), F32), pltpu.SemaphoreType.DMA((OUT_W_SLOTS,))],
        compiler_params=_params(("arbitrary", "arbitrary")),
        name="out_proj",
    )(x, a, w, g)


def _up_kernel(a_ref, r_ref, w_ref, o_ref):
    acc = _dot(a_ref[...], w_ref[...].astype(BF16))
    r = r_ref[...]
    for s in _chunks(o_ref.shape[1]):
        o_ref[:, s] = jnp.square(jnp.maximum(acc[:, s] * r, 0.0)).astype(o_ref.dtype)


def _up(xg, r, w):
    m, d = xg.shape
    f = w.shape[1]
    tm, tn = UP_TM, UP_TN
    return pl.pallas_call(
        _up_kernel,
        out_shape=jax.ShapeDtypeStruct((m, f), BF16),
        grid=(m // tm, f // tn),
        in_specs=[
            pl.BlockSpec((tm, d), lambda i, j: (i, 0)),
            pl.BlockSpec((tm, LANES), lambda i, j: (i, 0)),
            pl.BlockSpec((d, tn), lambda i, j: (0, j)),
        ],
        out_specs=pl.BlockSpec((tm, tn), lambda i, j: (i, j)),
        compiler_params=_params(("arbitrary", "arbitrary")),
        name="mlp_up",
    )(xg, r, w)


def _down_kernel(x_ref, u_ref, w_ref, o_ref):
    k = pl.program_id(2)

    @pl.when(k == 0)
    def _():
        o_ref[...] = x_ref[...]

    o_ref[...] += _dot(u_ref[...], w_ref[...].astype(BF16))


def _down(x, u, w):
    m, d = x.shape
    f = u.shape[1]
    tm, tn, tk = DOWN_TM, DOWN_TN, DOWN_TK
    return pl.pallas_call(
        _down_kernel,
        out_shape=jax.ShapeDtypeStruct((m, d), F32),
        grid=(m // tm, d // tn, f // tk),
        in_specs=[
            pl.BlockSpec((tm, tn), lambda i, j, k: (i, j)),
            pl.BlockSpec((tm, tk), lambda i, j, k: (i, k)),
            pl.BlockSpec((tk, tn), lambda i, j, k: (k, j)),
        ],
        out_specs=pl.BlockSpec((tm, tn), lambda i, j, k: (i, j)),
        compiler_params=_params(("arbitrary", "arbitrary", "arbitrary")),
        name="mlp_down",
    )(x, u, w)


def _rmsnorm_kernel(x_ref, g_ref, o_ref):
    x = x_ref[...]
    r = lax.rsqrt(_row_sumsq(x) * (1.0 / x.shape[1]) + NORM_EPS)
    for s in _chunks(o_ref.shape[1]):
        o_ref[:, s] = x[:, s] * r * g_ref[:, s]


def _rmsnorm(x, g):
    m, d = x.shape
    tm = ROW_TM
    return pl.pallas_call(
        _rmsnorm_kernel,
        out_shape=jax.ShapeDtypeStruct((m, d), F32),
        grid=(m // tm,),
        in_specs=[pl.BlockSpec((tm, d), lambda i: (i, 0)), pl.BlockSpec((1, d), lambda i: (0, 0))],
        out_specs=pl.BlockSpec((tm, d), lambda i: (i, 0)),
        compiler_params=_params(("arbitrary",)),
        name="final_rmsnorm",
    )(x, g)


def _base_tables(pos):
    n = pos.shape[0]
    rest = ATT_HEAD_DIM - ROPE_DIM
    rope_freq = ROPE_THETA ** (-jnp.arange(0, ROPE_DIM, 2, dtype=F32) / ROPE_DIM)
    ang = pos[:, None] * rope_freq[None, :]
    cos, sin = jnp.cos(ang), jnp.sin(ang)
    ones, zeros = jnp.ones((n, rest), F32), jnp.zeros((n, rest), F32)
    rep = LANES // ATT_HEAD_DIM
    rope = [jnp.tile(jnp.concatenate(parts, axis=1), (1, rep)) for parts in
            ([cos, cos, ones], [sin, sin, zeros], [-sin, sin, zeros])]
    ret_freq = 1.0 / (RET_THETA ** jnp.linspace(0.0, 1.0, RET_HEAD_DIM // 2, dtype=F32))
    ang = pos[:, None] * ret_freq[None, :]
    return jnp.concatenate(rope + [jnp.cos(ang), jnp.sin(ang)], axis=1)


def kernel(x, g_mix, w_in, att_sinks, beta_att, beta_ret, w_out, g_mlp, w_up, w_down, g_final):
    b, s, d = x.shape
    depth = w_in.shape[0]
    ta = _base_tables(jnp.arange(s // LANES, dtype=F32) * LANES)
    tb = _base_tables(jnp.arange(LANES, dtype=F32))
    log_g = jnp.log1p(-(2.0 ** (-5.0 - jnp.arange(RET_HEADS, dtype=F32))))

    outs = []
    for bi in range(b):
        xb = x[bi]
        for l in range(depth):
            proj, gate = _inproj(xb, g_mix[l][None, :], w_in[l], ta, tb, beta_ret[l][None, :])
            mixed = _mixer(proj, gate, att_sinks[l], log_g, beta_att[l][None, :])
            xb, xg, r = _outproj(xb, mixed, w_out[l], g_mlp[l][None, :])
            u = _up(xg, r, w_up[l])
            xb = _down(xb, u, w_down[l])
        outs.append(_rmsnorm(xb, g_final[None, :]))
    return outs[0][None] if b == 1 else jnp.stack(outs, axis=0)
```

```python
import functools
import math

import jax
import jax.numpy as jnp
from jax import lax
from jax.experimental import pallas as pl
from jax.experimental.pallas import tpu as pltpu

F32 = jnp.float32
BF16 = jnp.bfloat16

ATT_HEADS = 32
ATT_KV_HEADS = 4
ATT_HEAD_DIM = 64
ATT_GROUP = ATT_HEADS // ATT_KV_HEADS
ATT_WIDTH = ATT_HEADS * ATT_HEAD_DIM
ATT_KV_WIDTH = ATT_KV_HEADS * ATT_HEAD_DIM
ATT_BLOCK = 128
ROPE_DIM = ATT_HEAD_DIM // 4
ROPE_THETA = 500000.0
RET_HEADS = 8
RET_HEAD_DIM = 256
RET_WIDTH = RET_HEADS * RET_HEAD_DIM
RET_CHUNK = 128
RET_THETA = 10000.0
RET_EPS = 1e-6
NORM_EPS = 1e-5
LOG2E = math.log2(math.e)

LANES = 128
SUBLANES = 8
VMEM_LIMIT_BYTES = 60 * 1024 * 1024

OFF_AQ = 0
OFF_AK = OFF_AQ + ATT_WIDTH
OFF_AV = OFF_AK + ATT_KV_WIDTH
OFF_RQ = OFF_AV + ATT_KV_WIDTH
OFF_RK = OFF_RQ + RET_WIDTH
OFF_RV = OFF_RK + RET_WIDTH
OFF_RG = OFF_RV + RET_WIDTH
IN_COLS = OFF_RG + RET_WIDTH

POS_AQ = 0
POS_RQ = POS_AQ + ATT_WIDTH
POS_RK = POS_RQ + RET_WIDTH
POS_RV = POS_RK + RET_WIDTH
POS_KV = POS_RV + RET_WIDTH
PROJ_COLS = POS_KV + 2 * ATT_KV_WIDTH

IN_TM, IN_TN = 1024, 512
NORM_ROWS = 64
OUT_TM, OUT_TN = 1024, 512
W_SLOTS = 3
UP_TM, UP_TN = 1024, 512
DOWN_TM, DOWN_TN, DOWN_TK = 2048, 1024, 1024
ROW_TM = 512
ATT_LOOKAHEAD = 3
RET_LOOKAHEAD = 2
MIX_ROWS = 512


def _params(semantics):
    return pltpu.CompilerParams(dimension_semantics=semantics, vmem_limit_bytes=VMEM_LIMIT_BYTES)


def _dot(a, b):
    return jnp.dot(a, b, preferred_element_type=F32)


def _dot_t(a, b):
    return lax.dot_general(a, b, (((1,), (1,)), ((), ())), preferred_element_type=F32)


def _row_sumsq(x):
    return jnp.broadcast_to(jnp.sum(x * x, axis=-1, keepdims=True), (x.shape[0], LANES))


def _chunks(width):
    return [slice(c * LANES, (c + 1) * LANES) for c in range(width // LANES)]


def _ring_step(copy_for_step, step, n_steps):
    @pl.when(step == 0)
    def _():
        for s in range(W_SLOTS - 1):
            copy_for_step(s).start()

    @pl.when(step + (W_SLOTS - 1) < n_steps)
    def _():
        copy_for_step(step + (W_SLOTS - 1)).start()

    copy_for_step(step).wait()


def _norm_operands(x_ref, g_ref):
    x = x_ref[...]
    return (x * g_ref[...]).astype(BF16), lax.rsqrt(_row_sumsq(x) * (1.0 / x.shape[1]) + NORM_EPS)


def _store_norm_slice(operands, lhs_ref, r_ref, slot, step):
    xg, r = operands
    n_slices = lhs_ref.shape[1] // NORM_ROWS
    start = pl.multiple_of(jnp.minimum(step, n_slices - 1) * NORM_ROWS, NORM_ROWS)
    rows = pl.ds(start, NORM_ROWS)
    lhs_ref[slot, rows, :] = xg
    r_ref[slot, rows, :] = r


def _norm_slice_spec(m, tm, d):
    nb = m // tm
    n_slices = tm // NORM_ROWS

    def index(i, j):
        return jnp.where(i < nb, i * n_slices + jnp.minimum(j, n_slices - 1), nb * n_slices - 1), 0

    return pl.BlockSpec((NORM_ROWS, d), index)


def _inproj_kernel(x_ref, g_ref, w_ref, ta_ref, tb_ref, bret_ref, proj_ref, gate_ref,
                   lhs_ref, r_ref, tab_ref):
    i = pl.program_id(0)
    j = pl.program_id(1)
    tm, d = lhs_ref.shape[1:]
    tn = w_ref.shape[1]
    t_ak = OFF_AK // tn
    t_rq = OFF_RQ // tn
    t_rk = OFF_RK // tn
    t_rv = OFF_RV // tn
    t_rg = OFF_RG // tn
    c_ref, s_ref, cos_ref, sin_ref = (tab_ref.at[:, s] for s in _chunks(tab_ref.shape[1]))
    mul_slot = (i + 1) % 2
    norm_slot = i % 2

    @pl.when(i == 0)
    def _():
        _store_norm_slice(_norm_operands(x_ref, g_ref), lhs_ref, r_ref, norm_slot, j)

    multiplying = i > 0

    @pl.when(multiplying & (j == 0))
    def _():
        rc_b, rs_b, rss_b, tc_b, ts_b = (tb_ref[:, s] for s in _chunks(tb_ref.shape[1]))
        for c in range(tm // LANES):
            rows = slice(c * LANES, (c + 1) * LANES)
            rc_a, rs_a, rss_a, tc_a, ts_a = (ta_ref[c:c + 1, s] for s in _chunks(ta_ref.shape[1]))
            c_ref[rows, :] = rc_a * rc_b - rs_a * rs_b
            s_ref[rows, :] = rss_a * rc_b + rc_a * rss_b
            cos_ref[rows, :] = tc_a * tc_b - ts_a * ts_b
            sin_ref[rows, :] = ts_a * tc_b + tc_a * ts_b

    def tile(ref, epilogue):
        nxt = _norm_operands(x_ref, g_ref)
        acc = _dot(lhs_ref[mul_slot], w_ref[...].astype(BF16))
        r = r_ref[mul_slot]
        vals = epilogue([acc[:, s] * r for s in _chunks(tn)])
        for s, v in zip(_chunks(tn), vals):
            ref[:, s] = v.astype(ref.dtype)
        _store_norm_slice(nxt, lhs_ref, r_ref, norm_slot, j)

    def rope(x):
        lane = lax.broadcasted_iota(jnp.int32, x.shape, 1)
        first_half = (lane % ATT_HEAD_DIM) < ROPE_DIM // 2
        partner = jnp.where(first_half, pltpu.roll(x, LANES - ROPE_DIM // 2, 1), pltpu.roll(x, ROPE_DIM // 2, 1))
        return x * c_ref[...] + partner * s_ref[...]

    def rot_pairs(xs, scale):
        out = []
        for x1, x2 in zip(xs[0::2], xs[1::2]):
            cos, sin = cos_ref[...], sin_ref[...]
            out += [(x1 * cos - x2 * sin) * scale, (x2 * cos + x1 * sin) * scale]
        return out

    @pl.when(multiplying & (j < t_ak))
    def _():
        tile(proj_ref, lambda xs: [rope(x) * (ATT_HEAD_DIM ** -0.5 * LOG2E) for x in xs])

    @pl.when(multiplying & (j == t_ak))
    def _():
        nk = ATT_KV_WIDTH // LANES
        tile(proj_ref, lambda xs: [rope(x) for x in xs[:nk]] + xs[nk:])

    @pl.when(multiplying & (j >= t_rq) & (j < t_rk))
    def _():
        tile(proj_ref, lambda xs: rot_pairs(xs, 1.0))

    @pl.when(multiplying & (j >= t_rk) & (j < t_rv))
    def _():
        tile(proj_ref, lambda xs: rot_pairs(xs, RET_HEAD_DIM ** -0.5))

    @pl.when(multiplying & (j >= t_rv) & (j < t_rg))
    def _():
        tile(proj_ref, lambda xs: xs)

    @pl.when(multiplying & (j >= t_rg))
    def _():
        tile(gate_ref, lambda xs: [x * jax.nn.sigmoid(x) * bret_ref[:, s] for x, s in zip(xs, _chunks(tn))])


def _inproj(x, g, w, ta, tb, beta_ret):
    m, d = x.shape
    tm, tn = IN_TM, IN_TN
    assert 2 * ATT_KV_WIDTH == tn
    assert tb.shape[0] == LANES and ta.shape[0] * LANES == m
    nb = m // tm
    n_tiles = IN_COLS // tn
    n_slices = tm // NORM_ROWS
    assert n_slices <= n_tiles
    t_ak = OFF_AK // tn
    t_rg = OFF_RG // tn

    def proj_tile(i, j):
        moved = jnp.minimum(j, t_rg - 1) - 1
        tile = jnp.where(j < t_ak, j, jnp.where(j == t_ak, POS_KV // tn, moved))
        return jnp.maximum(i - 1, 0), jnp.where(i == 0, 0, tile)

    def gate_tile(i, j):
        return jnp.maximum(i - 1, 0), jnp.where(i == 0, 0, jnp.maximum(j - t_rg, 0))

    return pl.pallas_call(
        _inproj_kernel,
        out_shape=[jax.ShapeDtypeStruct((m, PROJ_COLS), BF16), jax.ShapeDtypeStruct((m, RET_WIDTH), F32)],
        grid=(nb + 1, n_tiles),
        in_specs=[
            _norm_slice_spec(m, tm, d),
            pl.BlockSpec((1, d), lambda i, j: (0, 0)),
            pl.BlockSpec((d, tn), lambda i, j: (0, jnp.where(i == 0, 0, j))),
            pl.BlockSpec((tm // LANES, ta.shape[1]), lambda i, j: (jnp.maximum(i - 1, 0), 0)),
            pl.BlockSpec(tb.shape, lambda i, j: (0, 0)),
            pl.BlockSpec((1, tn), lambda i, j: (0, jnp.maximum(j - t_rg, 0))),
        ],
        out_specs=[pl.BlockSpec((tm, tn), proj_tile), pl.BlockSpec((tm, tn), gate_tile)],
        scratch_shapes=[
            pltpu.VMEM((2, tm, d), BF16),
            pltpu.VMEM((2, tm, LANES), F32),
            pltpu.VMEM((tm, 4 * LANES), F32),
        ],
        compiler_params=_params(("arbitrary", "arbitrary")),
        name="in_proj",
    )(x, g, w, ta, tb, beta_ret)


def _attn_items(first_block, sink_ref, q_ref, kvc_ref, kvp_ref, beta_ref, o_ref):
    blk = ATT_BLOCK
    hd = ATT_HEAD_DIM
    jj = lax.broadcasted_iota(jnp.int32, (blk, blk), 0)
    ii = lax.broadcasted_iota(jnp.int32, (blk, blk), 1)
    upper = jj > ii
    lo_lanes = lax.broadcasted_iota(jnp.int32, (2 * blk, LANES), 1) < hd
    prev_bias = None if first_block is None else jnp.where(first_block, -jnp.inf, 0.0)

    v_all = jnp.concatenate([kvp_ref[:, ATT_KV_WIDTH:], kvc_ref[:, ATT_KV_WIDTH:]], axis=0)
    v_t = v_all.astype(F32).T.astype(BF16)

    keys = []
    for kc in range(ATT_KV_WIDTH // LANES):
        s = slice(kc * LANES, (kc + 1) * LANES)
        kcol = jnp.concatenate([kvp_ref[:, s], kvc_ref[:, s]], axis=0).astype(F32)
        kswap = pltpu.roll(kcol, hd, 1)
        for hh in range(2):
            own, other = (kcol, kswap) if hh == 0 else (kswap, kcol)
            keys.append((jnp.where(lo_lanes, own, 0.0).astype(BF16),
                         jnp.where(lo_lanes, 0.0, other).astype(BF16)))

    n_cols = ATT_WIDTH // LANES
    cols_per_kv = n_cols // ATT_KV_HEADS

    def scores(col):
        qc = q_ref[:, col * LANES:(col + 1) * LANES]
        return [_dot_t(kk, qc) for kk in keys[col // cols_per_kv]]

    pending = [scores(c) for c in range(ATT_LOOKAHEAD)]

    def item(col):
        sts = pending.pop(0)
        if col + ATT_LOOKAHEAD < n_cols:
            pending.append(scores(col + ATT_LOOKAHEAD))
        kvh = col // cols_per_kv
        cs = slice(col * LANES, (col + 1) * LANES)
        v_h = v_t[kvh * hd:(kvh + 1) * hd, :]
        outs = []
        for half, st in enumerate(sts):
            head = 2 * col + half
            prev = st[:blk] if prev_bias is None else st[:blk] + prev_bias
            comb = jnp.where(upper, prev, st[blk:])
            sink = sink_ref[head] * LOG2E
            m = jnp.maximum(jnp.max(comb, axis=0, keepdims=True), sink)
            p = jnp.exp2(comb - m)
            denom = jnp.sum(p, axis=0, keepdims=True) + jnp.exp2(sink - m)
            pp = jnp.concatenate([jnp.where(upper, p, 0.0), jnp.where(upper, 0.0, p)], axis=0)
            ot = _dot(v_h, pp.astype(BF16))
            outs.append(ot * (1.0 / denom))
        o_pair = jnp.concatenate(outs, axis=0).T
        o_ref[:, cs] = (o_pair * beta_ref[:, cs]).astype(o_ref.dtype)

    return [functools.partial(item, col) for col in range(n_cols)]


def _ret_items(q_ref, k_ref, v_ref, gate_ref, o_ref, state_ref, dmask_ref, xi_ref, zeta_ref, decay_ref):
    dh = RET_HEAD_DIM

    def first_products(h):
        hs = slice(h * dh, (h + 1) * dh)
        q = q_ref[:, hs]
        state = state_ref[h]
        return _dot_t(q, k_ref[:, hs]), _dot(q, state.astype(BF16)), state

    pending = [first_products(h) for h in range(RET_LOOKAHEAD)]

    def item(h):
        hs = slice(h * dh, (h + 1) * dh)
        qk, cross, state = pending.pop(0)
        if h + RET_LOOKAHEAD < RET_HEADS:
            pending.append(first_products(h + RET_LOOKAHEAD))
        k = k_ref[:, hs]
        v = v_ref[:, hs]
        xi = xi_ref[h]
        zeta = zeta_ref[h]

        a = qk * dmask_ref[h]
        inner = _dot(a.astype(BF16), v)
        o = inner + cross * jnp.concatenate([xi, xi], axis=1)

        mu = jnp.mean(o, axis=-1, keepdims=True)
        d = o - mu
        var = jnp.mean(d * d, axis=-1, keepdims=True)
        o = d * lax.rsqrt(var + RET_EPS)
        o_ref[:, hs] = (o * gate_ref[:, hs]).astype(o_ref.dtype)

        kz = (k.astype(F32) * jnp.concatenate([zeta, zeta], axis=1)).astype(BF16)
        kv = lax.dot_general(kz, v, (((0,), (0,)), ((), ())), preferred_element_type=F32)
        state_ref[h] = state * decay_ref[h][0:1, :] + kv

    return [functools.partial(item, h) for h in range(RET_HEADS)]


def _mixer_kernel(sink_ref, lg_ref, aq_ref, kv_ref, kvp_ref, rq_ref, rk_ref, rv_ref, gate_ref,
                  batt_ref, o_ref, state_ref, dmask_ref, xi_ref, zeta_ref, decay_ref):
    n = pl.program_id(0)
    ch = RET_CHUNK
    dh = RET_HEAD_DIM

    @pl.when(n == 0)
    def _():
        state_ref[...] = jnp.zeros_like(state_ref)
        ri = lax.broadcasted_iota(jnp.int32, (ch, ch), 0)
        ci = lax.broadcasted_iota(jnp.int32, (ch, ch), 1)
        diff = (ri - ci).astype(F32)
        row = ri.astype(F32)
        for h in range(RET_HEADS):
            lg = lg_ref[h]
            dmask_ref[h] = jnp.where(diff >= 0, jnp.exp(lg * jnp.maximum(diff, 0.0)), 0.0)
            xi_ref[h] = jnp.exp(lg * (row + 1.0))
            zeta_ref[h] = jnp.exp(lg * ((ch - 1.0) - row))
            decay_ref[h] = jnp.exp(jnp.full((SUBLANES, dh), lg * ch, F32))

    att_out = o_ref.at[:, :ATT_WIDTH]
    ret_out = o_ref.at[:, ATT_WIDTH:]
    for b in range(aq_ref.shape[0] // ch):
        rows = slice(b * ch, (b + 1) * ch)
        prev = kvp_ref if b == 0 else kv_ref.at[slice((b - 1) * ch, b * ch)]
        att = _attn_items(n == 0 if b == 0 else None, sink_ref, aq_ref.at[rows], kv_ref.at[rows], prev,
                          batt_ref, att_out.at[rows])
        ret = _ret_items(rq_ref.at[rows], rk_ref.at[rows], rv_ref.at[rows], gate_ref.at[rows],
                         ret_out.at[rows], state_ref, dmask_ref, xi_ref, zeta_ref, decay_ref)
        per_head = len(att) // len(ret)
        for h, ret_item in enumerate(ret):
            for att_item in att[h * per_head:(h + 1) * per_head]:
                att_item()
            ret_item()


def _mixer(proj, gate, sinks, log_g, beta_att):
    s = proj.shape[0]
    rows, ch, dh = MIX_ROWS, RET_CHUNK, RET_HEAD_DIM
    kv_w = 2 * ATT_KV_WIDTH
    wide = lambda pos: pl.BlockSpec((rows, RET_WIDTH), lambda n: (n, pos // RET_WIDTH))
    return pl.pallas_call(
        _mixer_kernel,
        out_shape=jax.ShapeDtypeStruct((s, ATT_WIDTH + RET_WIDTH), BF16),
        grid=(s // rows,),
        in_specs=[
            pl.BlockSpec(memory_space=pltpu.SMEM),
            pl.BlockSpec(memory_space=pltpu.SMEM),
            pl.BlockSpec((rows, ATT_WIDTH), lambda n: (n, POS_AQ // ATT_WIDTH)),
            pl.BlockSpec((rows, kv_w), lambda n: (n, POS_KV // kv_w)),
            pl.BlockSpec((ch, kv_w), lambda n: (jnp.maximum(n * (rows // ch) - 1, 0), POS_KV // kv_w)),
            wide(POS_RQ), wide(POS_RK), wide(POS_RV),
            pl.BlockSpec((rows, RET_WIDTH), lambda n: (n, 0)),
            pl.BlockSpec((1, ATT_WIDTH), lambda n: (0, 0)),
        ],
        out_specs=pl.BlockSpec((rows, ATT_WIDTH + RET_WIDTH), lambda n: (n, 0)),
        scratch_shapes=[
            pltpu.VMEM((RET_HEADS, dh, dh), F32),
            pltpu.VMEM((RET_HEADS, ch, ch), F32),
            pltpu.VMEM((RET_HEADS, ch, ch), F32),
            pltpu.VMEM((RET_HEADS, ch, ch), F32),
            pltpu.VMEM((RET_HEADS, SUBLANES, dh), F32),
        ],
        compiler_params=_params(("arbitrary",)),
        name="token_mixer",
    )(sinks, log_g, proj, proj, proj, proj, proj, proj, gate, beta_att)


def _outproj_kernel(x_ref, a_ref, w_hbm, g_ref, o_ref, xg_ref, r_ref, w_buf, w_sem, *, d_model):
    i = pl.program_id(0)
    j = pl.program_id(1)
    n_tiles = pl.num_programs(1)
    n_steps = pl.num_programs(0) * n_tiles
    step = i * n_tiles + j
    tn = w_buf.shape[2]

    def w_copy(s):
        col = pl.multiple_of(lax.rem(s, n_tiles) * tn, tn)
        slot = lax.rem(s, W_SLOTS)
        return pltpu.make_async_copy(w_hbm.at[:, pl.ds(col, tn)], w_buf.at[slot], w_sem.at[slot])

    _ring_step(w_copy, step, n_steps)
    w = w_buf[lax.rem(step, W_SLOTS)].astype(BF16)
    x2 = x_ref[...] + _dot(a_ref[...], w)
    o_ref[...] = x2
    xg_ref[...] = (x2 * g_ref[...]).astype(xg_ref.dtype)
    ss = _row_sumsq(x2)

    @pl.when(j == 0)
    def _():
        r_ref[...] = ss

    @pl.when(j > 0)
    def _():
        r_ref[...] += ss

    @pl.when(j == pl.num_programs(1) - 1)
    def _():
        r_ref[...] = lax.rsqrt(r_ref[...] * (1.0 / d_model) + NORM_EPS)


def _outproj(x, a, w, g):
    m, d = x.shape
    kdim = a.shape[1]
    tm, tn = OUT_TM, OUT_TN
    tile = lambda i, j: (i, j)
    row = lambda i, j: (i, 0)
    assert (m // tm) * (d // tn) >= W_SLOTS - 1
    return pl.pallas_call(
        functools.partial(_outproj_kernel, d_model=d),
        out_shape=[jax.ShapeDtypeStruct((m, d), F32), jax.ShapeDtypeStruct((m, d), BF16),
                   jax.ShapeDtypeStruct((m, LANES), F32)],
        grid=(m // tm, d // tn),
        in_specs=[
            pl.BlockSpec((tm, tn), tile),
            pl.BlockSpec((tm, kdim), row),
            pl.BlockSpec(memory_space=pl.ANY),
            pl.BlockSpec((1, tn), lambda i, j: (0, j)),
        ],
        out_specs=[pl.BlockSpec((tm, tn), tile), pl.BlockSpec((tm, tn), tile), pl.BlockSpec((tm, LANES), row)],
        scratch_shapes=[pltpu.VMEM((W_SLOTS, kdim, tn), F32), pltpu.SemaphoreType.DMA((W_SLOTS,))],
        compiler_params=_params(("arbitrary", "arbitrary")),
        name="out_proj",
    )(x, a, w, g)


def _up_kernel(a_ref, r_ref, w_hbm, o_ref, w_buf, w_sem):
    n_tiles = pl.num_programs(1)
    step = pl.program_id(0) * n_tiles + pl.program_id(1)
    tn = w_buf.shape[2]

    def w_copy(s):
        col = pl.multiple_of(lax.rem(s, n_tiles) * tn, tn)
        slot = lax.rem(s, W_SLOTS)
        return pltpu.make_async_copy(w_hbm.at[:, pl.ds(col, tn)], w_buf.at[slot], w_sem.at[slot])

    _ring_step(w_copy, step, pl.num_programs(0) * n_tiles)
    acc = _dot(a_ref[...], w_buf[lax.rem(step, W_SLOTS)].astype(BF16))
    r = r_ref[...]
    for s in _chunks(o_ref.shape[1]):
        o_ref[:, s] = jnp.square(jnp.maximum(acc[:, s] * r, 0.0)).astype(o_ref.dtype)


def _up(xg, r, w):
    m, d = xg.shape
    f = w.shape[1]
    tm, tn = UP_TM, UP_TN
    assert (m // tm) * (f // tn) >= W_SLOTS - 1
    return pl.pallas_call(
        _up_kernel,
        out_shape=jax.ShapeDtypeStruct((m, f), BF16),
        grid=(m // tm, f // tn),
        in_specs=[
            pl.BlockSpec((tm, d), lambda i, j: (i, 0)),
            pl.BlockSpec((tm, LANES), lambda i, j: (i, 0)),
            pl.BlockSpec(memory_space=pl.ANY),
        ],
        out_specs=pl.BlockSpec((tm, tn), lambda i, j: (i, j)),
        scratch_shapes=[pltpu.VMEM((W_SLOTS, d, tn), F32), pltpu.SemaphoreType.DMA((W_SLOTS,))],
        compiler_params=_params(("arbitrary", "arbitrary")),
        name="mlp_up",
    )(xg, r, w)


def _down_kernel(x_ref, u_ref, w_hbm, o_ref, w_buf, w_sem):
    k = pl.program_id(2)
    n_j, n_k = pl.num_programs(1), pl.num_programs(2)
    step = (pl.program_id(0) * n_j + pl.program_id(1)) * n_k + k
    tk, tn = w_buf.shape[1:]

    def w_copy(s):
        row = pl.multiple_of(lax.rem(s, n_k) * tk, tk)
        col = pl.multiple_of(lax.rem(lax.div(s, n_k), n_j) * tn, tn)
        slot = lax.rem(s, W_SLOTS)
        return pltpu.make_async_copy(w_hbm.at[pl.ds(row, tk), pl.ds(col, tn)], w_buf.at[slot], w_sem.at[slot])

    _ring_step(w_copy, step, pl.num_programs(0) * n_j * n_k)

    @pl.when(k == 0)
    def _():
        o_ref[...] = x_ref[...]

    o_ref[...] += _dot(u_ref[...], w_buf[lax.rem(step, W_SLOTS)].astype(BF16))


def _down(x, u, w):
    m, d = x.shape
    f = u.shape[1]
    tm, tn, tk = DOWN_TM, DOWN_TN, DOWN_TK
    assert (m // tm) * (d // tn) * (f // tk) >= W_SLOTS - 1
    return pl.pallas_call(
        _down_kernel,
        out_shape=jax.ShapeDtypeStruct((m, d), F32),
        grid=(m // tm, d // tn, f // tk),
        in_specs=[
            pl.BlockSpec((tm, tn), lambda i, j, k: (i, j)),
            pl.BlockSpec((tm, tk), lambda i, j, k: (i, k)),
            pl.BlockSpec(memory_space=pl.ANY),
        ],
        out_specs=pl.BlockSpec((tm, tn), lambda i, j, k: (i, j)),
        scratch_shapes=[pltpu.VMEM((W_SLOTS, tk, tn), F32), pltpu.SemaphoreType.DMA((W_SLOTS,))],
        compiler_params=_params(("arbitrary", "arbitrary", "arbitrary")),
        name="mlp_down",
    )(x, u, w)


def _rmsnorm_kernel(x_ref, g_ref, o_ref):
    x = x_ref[...]
    r = lax.rsqrt(_row_sumsq(x) * (1.0 / x.shape[1]) + NORM_EPS)
    for s in _chunks(o_ref.shape[1]):
        o_ref[:, s] = x[:, s] * r * g_ref[:, s]


def _rmsnorm(x, g):
    m, d = x.shape
    tm = ROW_TM
    return pl.pallas_call(
        _rmsnorm_kernel,
        out_shape=jax.ShapeDtypeStruct((m, d), F32),
        grid=(m // tm,),
        in_specs=[pl.BlockSpec((tm, d), lambda i: (i, 0)), pl.BlockSpec((1, d), lambda i: (0, 0))],
        out_specs=pl.BlockSpec((tm, d), lambda i: (i, 0)),
        compiler_params=_params(("arbitrary",)),
        name="final_rmsnorm",
    )(x, g)


def _base_tables(pos):
    n = pos.shape[0]
    rest = ATT_HEAD_DIM - ROPE_DIM
    rope_freq = ROPE_THETA ** (-jnp.arange(0, ROPE_DIM, 2, dtype=F32) / ROPE_DIM)
    ang = pos[:, None] * rope_freq[None, :]
    cos, sin = jnp.cos(ang), jnp.sin(ang)
    ones, zeros = jnp.ones((n, rest), F32), jnp.zeros((n, rest), F32)
    rep = LANES // ATT_HEAD_DIM
    rope = [jnp.tile(jnp.concatenate(parts, axis=1), (1, rep)) for parts in
            ([cos, cos, ones], [sin, sin, zeros], [-sin, sin, zeros])]
    ret_freq = 1.0 / (RET_THETA ** jnp.linspace(0.0, 1.0, RET_HEAD_DIM // 2, dtype=F32))
    ang = pos[:, None] * ret_freq[None, :]
    return jnp.concatenate(rope + [jnp.cos(ang), jnp.sin(ang)], axis=1)


def kernel(x, g_mix, w_in, att_sinks, beta_att, beta_ret, w_out, g_mlp, w_up, w_down, g_final):
    b, s, d = x.shape
    depth = w_in.shape[0]
    ta = _base_tables(jnp.arange(s // LANES, dtype=F32) * LANES)
    tb = _base_tables(jnp.arange(LANES, dtype=F32))
    log_g = jnp.log1p(-(2.0 ** (-5.0 - jnp.arange(RET_HEADS, dtype=F32))))

    outs = []
    for bi in range(b):
        xb = x[bi]
        for l in range(depth):
            proj, gate = _inproj(xb, g_mix[l][None, :], w_in[l], ta, tb, beta_ret[l][None, :])
            mixed = _mixer(proj, gate, att_sinks[l], log_g, beta_att[l][None, :])
            xb, xg, r = _outproj(xb, mixed, w_out[l], g_mlp[l][None, :])
            u = _up(xg, r, w_up[l])
            xb = _down(xb, u, w_down[l])
        outs.append(_rmsnorm(xb, g_final[None, :]))
    return outs[0][None] if b == 1 else jnp.stack(outs, axis=0)
```

```python
import functools
import math

import jax
import jax.numpy as jnp
from jax import lax
from jax.experimental import pallas as pl
from jax.experimental.pallas import tpu as pltpu

F32 = jnp.float32
BF16 = jnp.bfloat16

ATT_HEADS = 32
ATT_KV_HEADS = 4
ATT_HEAD_DIM = 64
ATT_GROUP = ATT_HEADS // ATT_KV_HEADS
ATT_WIDTH = ATT_HEADS * ATT_HEAD_DIM
ATT_KV_WIDTH = ATT_KV_HEADS * ATT_HEAD_DIM
ATT_BLOCK = 128
ROPE_DIM = ATT_HEAD_DIM // 4
ROPE_THETA = 500000.0
RET_HEADS = 8
RET_HEAD_DIM = 256
RET_WIDTH = RET_HEADS * RET_HEAD_DIM
RET_CHUNK = 128
RET_THETA = 10000.0
RET_EPS = 1e-6
NORM_EPS = 1e-5
LOG2E = math.log2(math.e)

LANES = 128
SUBLANES = 8
VMEM_LIMIT_BYTES = 60 * 1024 * 1024

OFF_AQ = 0
OFF_AK = OFF_AQ + ATT_WIDTH
OFF_AV = OFF_AK + ATT_KV_WIDTH
OFF_RQ = OFF_AV + ATT_KV_WIDTH
OFF_RK = OFF_RQ + RET_WIDTH
OFF_RV = OFF_RK + RET_WIDTH
OFF_RG = OFF_RV + RET_WIDTH
IN_COLS = OFF_RG + RET_WIDTH

POS_AQ = 0
POS_RQ = POS_AQ + ATT_WIDTH
POS_RK = POS_RQ + RET_WIDTH
POS_RV = POS_RK + RET_WIDTH
POS_KV = POS_RV + RET_WIDTH
PROJ_COLS = POS_KV + 2 * ATT_KV_WIDTH

IN_TM, IN_TN = 1024, 512
NORM_ROWS = 64
OUT_TM, OUT_TN = 1024, 512
W_SLOTS = 3
UP_TM, UP_TN = 1024, 512
DOWN_TM, DOWN_TN, DOWN_TK = 2048, 1024, 1024
ROW_TM = 512
ATT_LOOKAHEAD = 3
RET_LOOKAHEAD = 2
MIX_ROWS = 512


def _params(semantics):
    return pltpu.CompilerParams(dimension_semantics=semantics, vmem_limit_bytes=VMEM_LIMIT_BYTES)


def _dot(a, b):
    return jnp.dot(a, b, preferred_element_type=F32)


def _dot_t(a, b):
    return lax.dot_general(a, b, (((1,), (1,)), ((), ())), preferred_element_type=F32)


def _row_sumsq(x):
    return jnp.broadcast_to(jnp.sum(x * x, axis=-1, keepdims=True), (x.shape[0], LANES))


def _chunks(width):
    return [slice(c * LANES, (c + 1) * LANES) for c in range(width // LANES)]


def _ring_step(copy_for_step, step, n_steps):
    @pl.when(step == 0)
    def _():
        for s in range(W_SLOTS - 1):
            copy_for_step(s).start()

    @pl.when(step + (W_SLOTS - 1) < n_steps)
    def _():
        copy_for_step(step + (W_SLOTS - 1)).start()

    copy_for_step(step).wait()


def _norm_operands(x_ref, g_ref):
    x = x_ref[...]
    return (x * g_ref[...]).astype(BF16), lax.rsqrt(_row_sumsq(x) * (1.0 / x.shape[1]) + NORM_EPS)


def _store_norm_slice(operands, lhs_ref, r_ref, slot, step):
    xg, r = operands
    n_slices = lhs_ref.shape[1] // NORM_ROWS
    start = pl.multiple_of(jnp.minimum(step, n_slices - 1) * NORM_ROWS, NORM_ROWS)
    rows = pl.ds(start, NORM_ROWS)
    lhs_ref[slot, rows, :] = xg
    r_ref[slot, rows, :] = r


def _norm_slice_spec(m, tm, d):
    nb = m // tm
    n_slices = tm // NORM_ROWS

    def index(i, j):
        return jnp.where(i < nb, i * n_slices + jnp.minimum(j, n_slices - 1), nb * n_slices - 1), 0

    return pl.BlockSpec((NORM_ROWS, d), index)


def _inproj_kernel(x_ref, g_ref, w_hbm, ta_ref, tb_ref, bret_ref, proj_ref, gate_ref,
                   lhs_ref, r_ref, tab_ref, w_buf, w_sem):
    i = pl.program_id(0)
    j = pl.program_id(1)
    tm, d = lhs_ref.shape[1:]
    tn = w_buf.shape[2]
    t_ak = OFF_AK // tn
    t_rq = OFF_RQ // tn
    t_rk = OFF_RK // tn
    t_rv = OFF_RV // tn
    t_rg = OFF_RG // tn
    c_ref, s_ref, cos_ref, sin_ref = (tab_ref.at[:, s] for s in _chunks(tab_ref.shape[1]))
    mul_slot = (i + 1) % 2
    norm_slot = i % 2

    @pl.when(i == 0)
    def _():
        _store_norm_slice(_norm_operands(x_ref, g_ref), lhs_ref, r_ref, norm_slot, j)

    multiplying = i > 0
    n_tiles = pl.num_programs(1)
    mul_step = (i - 1) * n_tiles + j

    def w_copy(s):
        col = pl.multiple_of(lax.rem(s, n_tiles) * tn, tn)
        slot = lax.rem(s, W_SLOTS)
        return pltpu.make_async_copy(w_hbm.at[:, pl.ds(col, tn)], w_buf.at[slot], w_sem.at[slot])

    @pl.when(multiplying)
    def _():
        _ring_step(w_copy, mul_step, (pl.num_programs(0) - 1) * n_tiles)

    @pl.when(multiplying & (j == 0))
    def _():
        rc_b, rs_b, rss_b, tc_b, ts_b = (tb_ref[:, s] for s in _chunks(tb_ref.shape[1]))
        for c in range(tm // LANES):
            rows = slice(c * LANES, (c + 1) * LANES)
            rc_a, rs_a, rss_a, tc_a, ts_a = (ta_ref[c:c + 1, s] for s in _chunks(ta_ref.shape[1]))
            c_ref[rows, :] = rc_a * rc_b - rs_a * rs_b
            s_ref[rows, :] = rss_a * rc_b + rc_a * rss_b
            cos_ref[rows, :] = tc_a * tc_b - ts_a * ts_b
            sin_ref[rows, :] = ts_a * tc_b + tc_a * ts_b

    def tile(ref, epilogue):
        nxt = _norm_operands(x_ref, g_ref)
        acc = _dot(lhs_ref[mul_slot], w_buf[lax.rem(mul_step, W_SLOTS)].astype(BF16))
        r = r_ref[mul_slot]
        vals = epilogue([acc[:, s] * r for s in _chunks(tn)])
        for s, v in zip(_chunks(tn), vals):
            ref[:, s] = v.astype(ref.dtype)
        _store_norm_slice(nxt, lhs_ref, r_ref, norm_slot, j)

    def rope(x):
        lane = lax.broadcasted_iota(jnp.int32, x.shape, 1)
        first_half = (lane % ATT_HEAD_DIM) < ROPE_DIM // 2
        partner = jnp.where(first_half, pltpu.roll(x, LANES - ROPE_DIM // 2, 1), pltpu.roll(x, ROPE_DIM // 2, 1))
        return x * c_ref[...] + partner * s_ref[...]

    def rot_pairs(xs, scale):
        out = []
        for x1, x2 in zip(xs[0::2], xs[1::2]):
            cos, sin = cos_ref[...], sin_ref[...]
            out += [(x1 * cos - x2 * sin) * scale, (x2 * cos + x1 * sin) * scale]
        return out

    @pl.when(multiplying & (j < t_ak))
    def _():
        tile(proj_ref, lambda xs: [rope(x) * (ATT_HEAD_DIM ** -0.5 * LOG2E) for x in xs])

    @pl.when(multiplying & (j == t_ak))
    def _():
        nk = ATT_KV_WIDTH // LANES
        tile(proj_ref, lambda xs: [rope(x) for x in xs[:nk]] + xs[nk:])

    @pl.when(multiplying & (j >= t_rq) & (j < t_rk))
    def _():
        tile(proj_ref, lambda xs: rot_pairs(xs, 1.0))

    @pl.when(multiplying & (j >= t_rk) & (j < t_rv))
    def _():
        tile(proj_ref, lambda xs: rot_pairs(xs, RET_HEAD_DIM ** -0.5))

    @pl.when(multiplying & (j >= t_rv) & (j < t_rg))
    def _():
        tile(proj_ref, lambda xs: xs)

    @pl.when(multiplying & (j >= t_rg))
    def _():
        tile(gate_ref, lambda xs: [x * jax.nn.sigmoid(x) * bret_ref[:, s] for x, s in zip(xs, _chunks(tn))])


def _inproj(x, g, w, ta, tb, beta_ret):
    m, d = x.shape
    tm, tn = IN_TM, IN_TN
    assert 2 * ATT_KV_WIDTH == tn
    assert tb.shape[0] == LANES and ta.shape[0] * LANES == m
    nb = m // tm
    n_tiles = IN_COLS // tn
    n_slices = tm // NORM_ROWS
    assert n_slices <= n_tiles
    t_ak = OFF_AK // tn
    t_rg = OFF_RG // tn

    def proj_tile(i, j):
        moved = jnp.minimum(j, t_rg - 1) - 1
        tile = jnp.where(j < t_ak, j, jnp.where(j == t_ak, POS_KV // tn, moved))
        return jnp.maximum(i - 1, 0), jnp.where(i == 0, 0, tile)

    def gate_tile(i, j):
        return jnp.maximum(i - 1, 0), jnp.where(i == 0, 0, jnp.maximum(j - t_rg, 0))

    return pl.pallas_call(
        _inproj_kernel,
        out_shape=[jax.ShapeDtypeStruct((m, PROJ_COLS), BF16), jax.ShapeDtypeStruct((m, RET_WIDTH), F32)],
        grid=(nb + 1, n_tiles),
        in_specs=[
            _norm_slice_spec(m, tm, d),
            pl.BlockSpec((1, d), lambda i, j: (0, 0)),
            pl.BlockSpec(memory_space=pl.ANY),
            pl.BlockSpec((tm // LANES, ta.shape[1]), lambda i, j: (jnp.maximum(i - 1, 0), 0)),
            pl.BlockSpec(tb.shape, lambda i, j: (0, 0)),
            pl.BlockSpec((1, tn), lambda i, j: (0, jnp.maximum(j - t_rg, 0))),
        ],
        out_specs=[pl.BlockSpec((tm, tn), proj_tile), pl.BlockSpec((tm, tn), gate_tile)],
        scratch_shapes=[
            pltpu.VMEM((2, tm, d), BF16),
            pltpu.VMEM((2, tm, LANES), F32),
            pltpu.VMEM((tm, 4 * LANES), F32),
            pltpu.VMEM((W_SLOTS, d, tn), F32),
            pltpu.SemaphoreType.DMA((W_SLOTS,)),
        ],
        compiler_params=_params(("arbitrary", "arbitrary")),
        name="in_proj",
    )(x, g, w, ta, tb, beta_ret)


def _attn_items(first_block, sink_ref, q_ref, kvc_ref, kvp_ref, beta_ref, o_ref):
    blk = ATT_BLOCK
    hd = ATT_HEAD_DIM
    jj = lax.broadcasted_iota(jnp.int32, (blk, blk), 0)
    ii = lax.broadcasted_iota(jnp.int32, (blk, blk), 1)
    upper = jj > ii
    lo_lanes = lax.broadcasted_iota(jnp.int32, (2 * blk, LANES), 1) < hd
    prev_bias = None if first_block is None else jnp.where(first_block, -jnp.inf, 0.0)

    v_all = jnp.concatenate([kvp_ref[:, ATT_KV_WIDTH:], kvc_ref[:, ATT_KV_WIDTH:]], axis=0)
    v_t = v_all.astype(F32).T.astype(BF16)

    keys = []
    for kc in range(ATT_KV_WIDTH // LANES):
        s = slice(kc * LANES, (kc + 1) * LANES)
        kcol = jnp.concatenate([kvp_ref[:, s], kvc_ref[:, s]], axis=0).astype(F32)
        kswap = pltpu.roll(kcol, hd, 1)
        for hh in range(2):
            own, other = (kcol, kswap) if hh == 0 else (kswap, kcol)
            keys.append((jnp.where(lo_lanes, own, 0.0).astype(BF16),
                         jnp.where(lo_lanes, 0.0, other).astype(BF16)))

    n_cols = ATT_WIDTH // LANES
    cols_per_kv = n_cols // ATT_KV_HEADS

    def scores(col):
        qc = q_ref[:, col * LANES:(col + 1) * LANES]
        return [_dot_t(kk, qc) for kk in keys[col // cols_per_kv]]

    pending = [scores(c) for c in range(ATT_LOOKAHEAD)]

    def item(col):
        sts = pending.pop(0)
        if col + ATT_LOOKAHEAD < n_cols:
            pending.append(scores(col + ATT_LOOKAHEAD))
        kvh = col // cols_per_kv
        cs = slice(col * LANES, (col + 1) * LANES)
        v_h = v_t[kvh * hd:(kvh + 1) * hd, :]
        outs = []
        for half, st in enumerate(sts):
            head = 2 * col + half
            prev = st[:blk] if prev_bias is None else st[:blk] + prev_bias
            comb = jnp.where(upper, prev, st[blk:])
            sink = sink_ref[head] * LOG2E
            m = jnp.maximum(jnp.max(comb, axis=0, keepdims=True), sink)
            p = jnp.exp2(comb - m)
            denom = jnp.sum(p, axis=0, keepdims=True) + jnp.exp2(sink - m)
            pp = jnp.concatenate([jnp.where(upper, p, 0.0), jnp.where(upper, 0.0, p)], axis=0)
            ot = _dot(v_h, pp.astype(BF16))
            outs.append(ot * (1.0 / denom))
        o_pair = jnp.concatenate(outs, axis=0).T
        o_ref[:, cs] = (o_pair * beta_ref[:, cs]).astype(o_ref.dtype)

    return [functools.partial(item, col) for col in range(n_cols)]


def _ret_items(q_ref, k_ref, v_ref, gate_ref, o_ref, state_ref, dmask_ref, xi_ref, zeta_ref, decay_ref):
    dh = RET_HEAD_DIM

    def first_products(h):
        hs = slice(h * dh, (h + 1) * dh)
        q = q_ref[:, hs]
        state = state_ref[h]
        return _dot_t(q, k_ref[:, hs]), _dot(q, state.astype(BF16)), state

    pending = [first_products(h) for h in range(RET_LOOKAHEAD)]

    def item(h):
        hs = slice(h * dh, (h + 1) * dh)
        qk, cross, state = pending.pop(0)
        if h + RET_LOOKAHEAD < RET_HEADS:
            pending.append(first_products(h + RET_LOOKAHEAD))
        k = k_ref[:, hs]
        v = v_ref[:, hs]
        xi = xi_ref[h]
        zeta = zeta_ref[h]

        a = qk * dmask_ref[h]
        inner = _dot(a.astype(BF16), v)
        o = inner + cross * jnp.concatenate([xi, xi], axis=1)

        mu = jnp.mean(o, axis=-1, keepdims=True)
        d = o - mu
        var = jnp.mean(d * d, axis=-1, keepdims=True)
        o = d * lax.rsqrt(var + RET_EPS)
        o_ref[:, hs] = (o * gate_ref[:, hs]).astype(o_ref.dtype)

        kz = (k.astype(F32) * jnp.concatenate([zeta, zeta], axis=1)).astype(BF16)
        kv = lax.dot_general(kz, v, (((0,), (0,)), ((), ())), preferred_element_type=F32)
        state_ref[h] = state * decay_ref[h][0:1, :] + kv

    return [functools.partial(item, h) for h in range(RET_HEADS)]


def _mixer_kernel(sink_ref, lg_ref, aq_ref, kv_ref, kvp_ref, rq_ref, rk_ref, rv_ref, gate_ref,
                  batt_ref, o_ref, state_ref, dmask_ref, xi_ref, zeta_ref, decay_ref):
    n = pl.program_id(0)
    ch = RET_CHUNK
    dh = RET_HEAD_DIM

    @pl.when(n == 0)
    def _():
        state_ref[...] = jnp.zeros_like(state_ref)
        ri = lax.broadcasted_iota(jnp.int32, (ch, ch), 0)
        ci = lax.broadcasted_iota(jnp.int32, (ch, ch), 1)
        diff = (ri - ci).astype(F32)
        row = ri.astype(F32)
        for h in range(RET_HEADS):
            lg = lg_ref[h]
            dmask_ref[h] = jnp.where(diff >= 0, jnp.exp(lg * jnp.maximum(diff, 0.0)), 0.0)
            xi_ref[h] = jnp.exp(lg * (row + 1.0))
            zeta_ref[h] = jnp.exp(lg * ((ch - 1.0) - row))
            decay_ref[h] = jnp.exp(jnp.full((SUBLANES, dh), lg * ch, F32))

    att_out = o_ref.at[:, :ATT_WIDTH]
    ret_out = o_ref.at[:, ATT_WIDTH:]
    for b in range(aq_ref.shape[0] // ch):
        rows = slice(b * ch, (b + 1) * ch)
        prev = kvp_ref if b == 0 else kv_ref.at[slice((b - 1) * ch, b * ch)]
        att = _attn_items(n == 0 if b == 0 else None, sink_ref, aq_ref.at[rows], kv_ref.at[rows], prev,
                          batt_ref, att_out.at[rows])
        ret = _ret_items(rq_ref.at[rows], rk_ref.at[rows], rv_ref.at[rows], gate_ref.at[rows],
                         ret_out.at[rows], state_ref, dmask_ref, xi_ref, zeta_ref, decay_ref)
        per_head = len(att) // len(ret)
        for h, ret_item in enumerate(ret):
            for att_item in att[h * per_head:(h + 1) * per_head]:
                att_item()
            ret_item()


def _mixer(proj, gate, sinks, log_g, beta_att):
    s = proj.shape[0]
    rows, ch, dh = MIX_ROWS, RET_CHUNK, RET_HEAD_DIM
    kv_w = 2 * ATT_KV_WIDTH
    wide = lambda pos: pl.BlockSpec((rows, RET_WIDTH), lambda n: (n, pos // RET_WIDTH))
    return pl.pallas_call(
        _mixer_kernel,
        out_shape=jax.ShapeDtypeStruct((s, ATT_WIDTH + RET_WIDTH), BF16),
        grid=(s // rows,),
        in_specs=[
            pl.BlockSpec(memory_space=pltpu.SMEM),
            pl.BlockSpec(memory_space=pltpu.SMEM),
            pl.BlockSpec((rows, ATT_WIDTH), lambda n: (n, POS_AQ // ATT_WIDTH)),
            pl.BlockSpec((rows, kv_w), lambda n: (n, POS_KV // kv_w)),
            pl.BlockSpec((ch, kv_w), lambda n: (jnp.maximum(n * (rows // ch) - 1, 0), POS_KV // kv_w)),
            wide(POS_RQ), wide(POS_RK), wide(POS_RV),
            pl.BlockSpec((rows, RET_WIDTH), lambda n: (n, 0)),
            pl.BlockSpec((1, ATT_WIDTH), lambda n: (0, 0)),
        ],
        out_specs=pl.BlockSpec((rows, ATT_WIDTH + RET_WIDTH), lambda n: (n, 0)),
        scratch_shapes=[
            pltpu.VMEM((RET_HEADS, dh, dh), F32),
            pltpu.VMEM((RET_HEADS, ch, ch), F32),
            pltpu.VMEM((RET_HEADS, ch, ch), F32),
            pltpu.VMEM((RET_HEADS, ch, ch), F32),
            pltpu.VMEM((RET_HEADS, SUBLANES, dh), F32),
        ],
        compiler_params=_params(("arbitrary",)),
        name="token_mixer",
    )(sinks, log_g, proj, proj, proj, proj, proj, proj, gate, beta_att)


def _outproj_kernel(x_ref, a_ref, w_hbm, g_ref, o_ref, xg_ref, r_ref, w_buf, w_sem, *, d_model):
    i = pl.program_id(0)
    j = pl.program_id(1)
    n_tiles = pl.num_programs(1)
    n_steps = pl.num_programs(0) * n_tiles
    step = i * n_tiles + j
    tn = w_buf.shape[2]

    def w_copy(s):
        col = pl.multiple_of(lax.rem(s, n_tiles) * tn, tn)
        slot = lax.rem(s, W_SLOTS)
        return pltpu.make_async_copy(w_hbm.at[:, pl.ds(col, tn)], w_buf.at[slot], w_sem.at[slot])

    _ring_step(w_copy, step, n_steps)
    w = w_buf[lax.rem(step, W_SLOTS)].astype(BF16)
    x2 = x_ref[...] + _dot(a_ref[...], w)
    o_ref[...] = x2
    xg_ref[...] = (x2 * g_ref[...]).astype(xg_ref.dtype)
    ss = _row_sumsq(x2)

    @pl.when(j == 0)
    def _():
        r_ref[...] = ss

    @pl.when(j > 0)
    def _():
        r_ref[...] += ss

    @pl.when(j == pl.num_programs(1) - 1)
    def _():
        r_ref[...] = lax.rsqrt(r_ref[...] * (1.0 / d_model) + NORM_EPS)


def _outproj(x, a, w, g):
    m, d = x.shape
    kdim = a.shape[1]
    tm, tn = OUT_TM, OUT_TN
    tile = lambda i, j: (i, j)
    row = lambda i, j: (i, 0)
    assert (m // tm) * (d // tn) >= W_SLOTS - 1
    return pl.pallas_call(
        functools.partial(_outproj_kernel, d_model=d),
        out_shape=[jax.ShapeDtypeStruct((m, d), F32), jax.ShapeDtypeStruct((m, d), BF16),
                   jax.ShapeDtypeStruct((m, LANES), F32)],
        grid=(m // tm, d // tn),
        in_specs=[
            pl.BlockSpec((tm, tn), tile),
            pl.BlockSpec((tm, kdim), row),
            pl.BlockSpec(memory_space=pl.ANY),
            pl.BlockSpec((1, tn), lambda i, j: (0, j)),
        ],
        out_specs=[pl.BlockSpec((tm, tn), tile), pl.BlockSpec((tm, tn), tile), pl.BlockSpec((tm, LANES), row)],
        scratch_shapes=[pltpu.VMEM((W_SLOTS, kdim, tn), F32), pltpu.SemaphoreType.DMA((W_SLOTS,))],
        compiler_params=_params(("arbitrary", "arbitrary")),
        name="out_proj",
    )(x, a, w, g)


def _up_kernel(a_ref, r_ref, w_hbm, o_ref, w_buf, w_sem):
    n_tiles = pl.num_programs(1)
    step = pl.program_id(0) * n_tiles + pl.program_id(1)
    tn = w_buf.shape[2]

    def w_copy(s):
        col = pl.multiple_of(lax.rem(s, n_tiles) * tn, tn)
        slot = lax.rem(s, W_SLOTS)
        return pltpu.make_async_copy(w_hbm.at[:, pl.ds(col, tn)], w_buf.at[slot], w_sem.at[slot])

    _ring_step(w_copy, step, pl.num_programs(0) * n_tiles)
    acc = _dot(a_ref[...], w_buf[lax.rem(step, W_SLOTS)].astype(BF16))
    r = r_ref[...]
    for s in _chunks(o_ref.shape[1]):
        o_ref[:, s] = jnp.square(jnp.maximum(acc[:, s] * r, 0.0)).astype(o_ref.dtype)


def _up(xg, r, w):
    m, d = xg.shape
    f = w.shape[1]
    tm, tn = UP_TM, UP_TN
    assert (m // tm) * (f // tn) >= W_SLOTS - 1
    return pl.pallas_call(
        _up_kernel,
        out_shape=jax.ShapeDtypeStruct((m, f), BF16),
        grid=(m // tm, f // tn),
        in_specs=[
            pl.BlockSpec((tm, d), lambda i, j: (i, 0)),
            pl.BlockSpec((tm, LANES), lambda i, j: (i, 0)),
            pl.BlockSpec(memory_space=pl.ANY),
        ],
        out_specs=pl.BlockSpec((tm, tn), lambda i, j: (i, j)),
        scratch_shapes=[pltpu.VMEM((W_SLOTS, d, tn), F32), pltpu.SemaphoreType.DMA((W_SLOTS,))],
        compiler_params=_params(("arbitrary", "arbitrary")),
        name="mlp_up",
    )(xg, r, w)


def _down_kernel(x_ref, u_ref, w_hbm, o_ref, w_buf, w_sem):
    k = pl.program_id(2)
    n_j, n_k = pl.num_programs(1), pl.num_programs(2)
    step = (pl.program_id(0) * n_j + pl.program_id(1)) * n_k + k
    tk, tn = w_buf.shape[1:]

    def w_copy(s):
        row = pl.multiple_of(lax.rem(s, n_k) * tk, tk)
        col = pl.multiple_of(lax.rem(lax.div(s, n_k), n_j) * tn, tn)
        slot = lax.rem(s, W_SLOTS)
        return pltpu.make_async_copy(w_hbm.at[pl.ds(row, tk), pl.ds(col, tn)], w_buf.at[slot], w_sem.at[slot])

    _ring_step(w_copy, step, pl.num_programs(0) * n_j * n_k)

    @pl.when(k == 0)
    def _():
        o_ref[...] = x_ref[...]

    o_ref[...] += _dot(u_ref[...], w_buf[lax.rem(step, W_SLOTS)].astype(BF16))


def _down(x, u, w):
    m, d = x.shape
    f = u.shape[1]
    tm, tn, tk = DOWN_TM, DOWN_TN, DOWN_TK
    assert (m // tm) * (d // tn) * (f // tk) >= W_SLOTS - 1
    return pl.pallas_call(
        _down_kernel,
        out_shape=jax.ShapeDtypeStruct((m, d), F32),
        grid=(m // tm, d // tn, f // tk),
        in_specs=[
            pl.BlockSpec((tm, tn), lambda i, j, k: (i, j)),
            pl.BlockSpec((tm, tk), lambda i, j, k: (i, k)),
            pl.BlockSpec(memory_space=pl.ANY),
        ],
        out_specs=pl.BlockSpec((tm, tn), lambda i, j, k: (i, j)),
        scratch_shapes=[pltpu.VMEM((W_SLOTS, tk, tn), F32), pltpu.SemaphoreType.DMA((W_SLOTS,))],
        compiler_params=_params(("arbitrary", "arbitrary", "arbitrary")),
        name="mlp_down",
    )(x, u, w)


def _rmsnorm_kernel(x_ref, g_ref, o_ref):
    x = x_ref[...]
    r = lax.rsqrt(_row_sumsq(x) * (1.0 / x.shape[1]) + NORM_EPS)
    for s in _chunks(o_ref.shape[1]):
        o_ref[:, s] = x[:, s] * r * g_ref[:, s]


def _rmsnorm(x, g):
    m, d = x.shape
    tm = ROW_TM
    return pl.pallas_call(
        _rmsnorm_kernel,
        out_shape=jax.ShapeDtypeStruct((m, d), F32),
        grid=(m // tm,),
        in_specs=[pl.BlockSpec((tm, d), lambda i: (i, 0)), pl.BlockSpec((1, d), lambda i: (0, 0))],
        out_specs=pl.BlockSpec((tm, d), lambda i: (i, 0)),
        compiler_params=_params(("arbitrary",)),
        name="final_rmsnorm",
    )(x, g)


def _base_tables(pos):
    n = pos.shape[0]
    rest = ATT_HEAD_DIM - ROPE_DIM
    rope_freq = ROPE_THETA ** (-jnp.arange(0, ROPE_DIM, 2, dtype=F32) / ROPE_DIM)
    ang = pos[:, None] * rope_freq[None, :]
    cos, sin = jnp.cos(ang), jnp.sin(ang)
    ones, zeros = jnp.ones((n, rest), F32), jnp.zeros((n, rest), F32)
    rep = LANES // ATT_HEAD_DIM
    rope = [jnp.tile(jnp.concatenate(parts, axis=1), (1, rep)) for parts in
            ([cos, cos, ones], [sin, sin, zeros], [-sin, sin, zeros])]
    ret_freq = 1.0 / (RET_THETA ** jnp.linspace(0.0, 1.0, RET_HEAD_DIM // 2, dtype=F32))
    ang = pos[:, None] * ret_freq[None, :]
    return jnp.concatenate(rope + [jnp.cos(ang), jnp.sin(ang)], axis=1)


def kernel(x, g_mix, w_in, att_sinks, beta_att, beta_ret, w_out, g_mlp, w_up, w_down, g_final):
    b, s, d = x.shape
    depth = w_in.shape[0]
    ta = _base_tables(jnp.arange(s // LANES, dtype=F32) * LANES)
    tb = _base_tables(jnp.arange(LANES, dtype=F32))
    log_g = jnp.log1p(-(2.0 ** (-5.0 - jnp.arange(RET_HEADS, dtype=F32))))

    outs = []
    for bi in range(b):
        xb = x[bi]
        for l in range(depth):
            proj, gate = _inproj(xb, g_mix[l][None, :], w_in[l], ta, tb, beta_ret[l][None, :])
            mixed = _mixer(proj, gate, att_sinks[l], log_g, beta_att[l][None, :])
            xb, xg, r = _outproj(xb, mixed, w_out[l], g_mlp[l][None, :])
            u = _up(xg, r, w_up[l])
            xb = _down(xb, u, w_down[l])
        outs.append(_rmsnorm(xb, g_final[None, :]))
    return outs[0][None] if b == 1 else jnp.stack(outs, axis=0)
```
